```python
import math
import jax
import jax.numpy as jnp
from jax import lax
import numpy as np

D_MODEL = 2048
BATCH = 4
SEQ = 4096
DEPTH = 2
DEC_BATCH = 128
DEC_SEQ = 8
PAST_LEN = 16384
PAGE_SIZE = 128

N_MIXERS = 2
N_A_LAYERS = (DEPTH + N_MIXERS - 1) // N_MIXERS
N_B_LAYERS = DEPTH // N_MIXERS

MIX_WIDTH = D_MODEL
N_MEM = 256
MEM_HEADS = 4
MEM_HEAD_DIM = MIX_WIDTH // 16
MEM_WIDTH = MEM_HEADS * MEM_HEAD_DIM
MAIN_WIDTH = MIX_WIDTH - MEM_WIDTH

MLA_HEADS = 12
MLA_V_DIM = MAIN_WIDTH // MLA_HEADS
MLA_NOPE_DIM = 128
MLA_ROPE_DIM = 64
MLA_Q_RANK = 512
MLA_KV_RANK = 512
MLA_CACHE_DIM = MLA_KV_RANK + MLA_ROPE_DIM
MLA_IN_COLS = MLA_Q_RANK + MLA_KV_RANK + MLA_ROPE_DIM + MEM_WIDTH
MLA_SCALE = (MLA_NOPE_DIM + MLA_ROPE_DIM) ** -0.5

DIFF_HEADS = 12
DIFF_HEAD_DIM = MAIN_WIDTH // (2 * DIFF_HEADS)
DIFF_KV_HEADS = 2
DIFF_REP = DIFF_HEADS // DIFF_KV_HEADS
DIFF_V_DIM = 2 * DIFF_HEAD_DIM
DIFF_Q_COLS = DIFF_HEADS * 2 * DIFF_HEAD_DIM
DIFF_K_COLS = DIFF_KV_HEADS * 2 * DIFF_HEAD_DIM
DIFF_V_COLS = DIFF_KV_HEADS * DIFF_V_DIM
DIFF_IN_COLS = DIFF_Q_COLS + DIFF_K_COLS + DIFF_V_COLS + MEM_WIDTH
DIFF_SCALE = DIFF_HEAD_DIM ** -0.5

MEM_SCALE = MEM_HEAD_DIM ** -0.5
D_FF = 4 * D_MODEL
ROPE_THETA = 10000.0
NORM_EPS = 1e-6
Q_BLOCK = 128
NEG_INF = -1e30

kernel_name = 'hybrid_mla_diffattn_memory_decoder_step'


def rms_norm(x, g):
    xf = x.astype(jnp.float32)
    y = xf * lax.rsqrt(jnp.mean(xf * xf, axis=-1, keepdims=True) + NORM_EPS)
    return (y * g.astype(jnp.float32)).astype(x.dtype)


def rope(x, pos):
    half = x.shape[-1] // 2
    inv_freq = jnp.exp(-math.log(ROPE_THETA) * jnp.arange(half, dtype=jnp.float32) / half)
    ang = pos.astype(jnp.float32)[:, None] * inv_freq[None, :]
    ang = ang.reshape((ang.shape[0],) + (1,) * (x.ndim - 3) + (half,))
    cos, sin = jnp.cos(ang), jnp.sin(ang)
    xf = x.astype(jnp.float32)
    x1, x2 = xf[..., :half], xf[..., half:]
    return jnp.concatenate([x1 * cos - x2 * sin, x1 * sin + x2 * cos], axis=-1).astype(x.dtype)


def lambda_init(layer):
    return 0.8 - 0.6 * math.exp(-0.3 * layer)


def causal_attention(q, k, v, scale):
    b, s, g, r, dk = q.shape
    n_blk = s // Q_BLOCK
    q_blocks = jnp.moveaxis(q.reshape(b, n_blk, Q_BLOCK, g, r, dk), 1, 0)
    k_pos = jnp.arange(s)

    def one_block(args):
        blk, q_blk = args
        sc = jnp.einsum('bqgrd,bkgd->bgrqk', q_blk, k, preferred_element_type=jnp.float32) * scale
        q_pos = blk * Q_BLOCK + jnp.arange(Q_BLOCK)
        sc = jnp.where(k_pos[None, :] <= q_pos[:, None], sc, NEG_INF)
        p = jax.nn.softmax(sc, axis=-1)
        return jnp.einsum('bgrqk,bkgd->bqgrd', p.astype(v.dtype), v)

    out = lax.map(one_block, (jnp.arange(n_blk), q_blocks))
    return jnp.moveaxis(out, 0, 1).reshape(b, s, g, r, v.shape[-1])


def paged_attention(q, fetch, page_table, k_new, v_new, scale):
    bd, t, g, r, _ = q.shape
    dv = v_new.shape[-1]

    def update(carry, k_blk, v_blk, mask):
        m, l, acc = carry
        sc = jnp.einsum('btgrd,bkgd->bgrtk', q, k_blk, preferred_element_type=jnp.float32) * scale
        if mask is not None:
            sc = jnp.where(mask, sc, NEG_INF)
        m_new = jnp.maximum(m, jnp.max(sc, axis=-1))
        alpha = jnp.exp(m - m_new)
        p = jnp.exp(sc - m_new[..., None])
        l = l * alpha + jnp.sum(p, axis=-1)
        acc = acc * alpha[..., None] + jnp.einsum('bgrtk,bkgd->bgrtd', p, v_blk.astype(jnp.float32))
        return (m_new, l, acc)

    init = (jnp.full((bd, g, r, t), NEG_INF, jnp.float32),
            jnp.zeros((bd, g, r, t), jnp.float32),
            jnp.zeros((bd, g, r, t, dv), jnp.float32))

    def step(carry, phys):
        k_blk, v_blk = fetch(phys)
        return update(carry, k_blk, v_blk, None), None

    carry, _ = lax.scan(step, init, page_table.T)
    causal = jnp.arange(t)[None, :] <= jnp.arange(t)[:, None]
    m, l, acc = update(carry, k_new, v_new, causal)
    out = acc / l[..., None]
    return jnp.transpose(out, (0, 3, 1, 2, 4)).astype(q.dtype)


def memory_kv(mem, g_mem, w_mem_kv):
    b = mem.shape[0]
    hm = rms_norm(mem[None], g_mem[:, None, None, :])
    kv = jnp.einsum('lbmd,lde->lbme', hm, w_mem_kv).reshape(DEPTH, b, N_MEM, 2, MEM_HEADS, MEM_HEAD_DIM)
    return kv[:, :, :, 0], kv[:, :, :, 1]


def memory_attention(q, mem_k, mem_v):
    sc = jnp.einsum('bshd,bmhd->bhsm', q, mem_k, preferred_element_type=jnp.float32) * MEM_SCALE
    p = jax.nn.softmax(sc, axis=-1)
    return jnp.einsum('bhsm,bmhd->bshd', p.astype(mem_v.dtype), mem_v)


def mla_project(h, pos, w_in, g_q, g_kv, w_uq):
    b, s, _ = h.shape
    q_a, kv_a, k_r, mem_q = jnp.split(
        h @ w_in, [MLA_Q_RANK, MLA_Q_RANK + MLA_KV_RANK, MLA_Q_RANK + MLA_KV_RANK + MLA_ROPE_DIM], axis=-1)
    q = (rms_norm(q_a, g_q) @ w_uq).reshape(b, s, MLA_HEADS, MLA_NOPE_DIM + MLA_ROPE_DIM)
    q_nope, q_rope = q[..., :MLA_NOPE_DIM], rope(q[..., MLA_NOPE_DIM:], pos)
    rows = jnp.concatenate([rms_norm(kv_a, g_kv), rope(k_r, pos)], axis=-1)
    return q_nope, q_rope, rows, mem_q


def mla_attend_prompt(q_nope, q_rope, rows, w_uk, w_uv):
    b, s, _ = rows.shape
    c_kv, k_rope = rows[..., :MLA_KV_RANK], rows[..., MLA_KV_RANK:]
    k_nope = jnp.einsum('bsc,chd->bshd', c_kv, w_uk)
    v = jnp.einsum('bsc,chd->bshd', c_kv, w_uv)
    k = jnp.concatenate(
        [k_nope, jnp.broadcast_to(k_rope[:, :, None, :], (b, s, MLA_HEADS, MLA_ROPE_DIM))], axis=-1)
    q = jnp.concatenate([q_nope, q_rope], axis=-1)[:, :, :, None, :]
    o = causal_attention(q, k, v, MLA_SCALE)
    return o.reshape(b, s, MLA_HEADS * MLA_V_DIM)


def mla_attend_sample(q_nope, q_rope, rows, w_uk, w_uv, cache_mla, layer, page_table):
    bd, t, _ = rows.shape
    q_lat = jnp.einsum('bthd,chd->bthc', q_nope, w_uk)
    q = jnp.concatenate([q_lat, q_rope], axis=-1)[:, :, None]

    def fetch(phys):
        blk = cache_mla[layer, phys]
        return blk[:, :, None, :], blk[:, :, None, :MLA_KV_RANK]

    o_lat = paged_attention(q, fetch, page_table, rows[:, :, None, :],
                            rows[:, :, None, :MLA_KV_RANK], MLA_SCALE)
    o = jnp.einsum('bthc,chd->bthd', o_lat[:, :, 0], w_uv)
    return o.reshape(bd, t, MLA_HEADS * MLA_V_DIM)


def diff_project(h, pos, w_in):
    b, s, _ = h.shape
    q, k, v, mem_q = jnp.split(
        h @ w_in, [DIFF_Q_COLS, DIFF_Q_COLS + DIFF_K_COLS, DIFF_Q_COLS + DIFF_K_COLS + DIFF_V_COLS], axis=-1)
    q = rope(q.reshape(b, s, DIFF_KV_HEADS, DIFF_REP, 2, DIFF_HEAD_DIM), pos)
    k = rope(k.reshape(b, s, DIFF_KV_HEADS, 2, DIFF_HEAD_DIM), pos)
    v = v.reshape(b, s, DIFF_KV_HEADS, DIFF_V_DIM)
    return q, k, v, mem_q


def dup_maps(v):
    b, s = v.shape[:2]
    return jnp.broadcast_to(v[:, :, :, None, :], (b, s, DIFF_KV_HEADS, 2, DIFF_V_DIM)).reshape(
        b, s, 2 * DIFF_KV_HEADS, DIFF_V_DIM)


def diff_maps(q, k, v):
    b, s = q.shape[:2]
    qm = jnp.swapaxes(q, 3, 4).reshape(b, s, 2 * DIFF_KV_HEADS, DIFF_REP, DIFF_HEAD_DIM)
    km = k.reshape(b, s, 2 * DIFF_KV_HEADS, DIFF_HEAD_DIM)
    return qm, km, dup_maps(v)


def diff_attend_sample(qm, km, vm, cache_k, cache_v, layer, page_table):
    def fetch(phys):
        bd = phys.shape[0]
        kb = cache_k[layer, phys].reshape(bd, PAGE_SIZE, 2 * DIFF_KV_HEADS, DIFF_HEAD_DIM)
        return kb, dup_maps(cache_v[layer, phys])

    return paged_attention(qm, fetch, page_table, km, vm, DIFF_SCALE)


def diff_combine(o, lam_p, g_sub, lam_init):
    b, s = o.shape[:2]
    o = o.reshape(b, s, DIFF_KV_HEADS, 2, DIFF_REP, DIFF_V_DIM)
    lp = lam_p.astype(jnp.float32)
    lam = jnp.exp(jnp.sum(lp[0] * lp[1])) - jnp.exp(jnp.sum(lp[2] * lp[3])) + lam_init
    d = o[:, :, :, 0] - lam.astype(o.dtype) * o[:, :, :, 1]
    d = rms_norm(d, g_sub) * (1.0 - lam_init)
    return d.reshape(b, s, DIFF_HEADS * DIFF_V_DIM)


def setup_inputs(seed: int = 0) -> dict:
    key = jax.random.key(seed)
    ks = jax.random.split(key, 24)
    n_pages = PAST_LEN // PAGE_SIZE
    n_used = DEC_BATCH * n_pages
    n_phys = n_used + n_used // 4

    def nrm(k, shape, scale=1.0):
        return jax.random.normal(k, shape, jnp.float32) * scale

    def gain(k, shape):
        return 1.0 + 0.05 * jax.random.normal(k, shape, jnp.float32)

    page_table = jax.random.permutation(ks[8], n_phys)[:n_used].reshape(DEC_BATCH, n_pages).astype(jnp.int32)
    return {
        'x_prompt': nrm(ks[0], (BATCH, SEQ, D_MODEL)),
        'x_sample': nrm(ks[1], (DEC_BATCH, DEC_SEQ, D_MODEL)),
        'mem_prompt': nrm(ks[2], (BATCH, N_MEM, D_MODEL)),
        'cache_mla': nrm(ks[3], (N_A_LAYERS, n_phys, PAGE_SIZE, MLA_CACHE_DIM)),
        'cache_diff_k': nrm(ks[4], (N_B_LAYERS, n_phys, PAGE_SIZE, DIFF_KV_HEADS, 2, DIFF_HEAD_DIM)),
        'cache_diff_v': nrm(ks[5], (N_B_LAYERS, n_phys, PAGE_SIZE, DIFF_KV_HEADS, DIFF_V_DIM)),
        'cache_mem_k': nrm(ks[6], (DEPTH, DEC_BATCH, N_MEM, MEM_HEADS, MEM_HEAD_DIM)),
        'cache_mem_v': nrm(ks[7], (DEPTH, DEC_BATCH, N_MEM, MEM_HEADS, MEM_HEAD_DIM)),
        'page_table': page_table,
        'norm_gains': gain(ks[9], (DEPTH, 4, D_MODEL)),
        'w_mix_in_a': nrm(ks[10], (N_A_LAYERS, D_MODEL, MLA_IN_COLS), D_MODEL ** -0.5),
        'g_mla_q': gain(ks[11], (N_A_LAYERS, MLA_Q_RANK)),
        'g_mla_kv': gain(ks[12], (N_A_LAYERS, MLA_KV_RANK)),
        'w_mla_uq': nrm(ks[13], (N_A_LAYERS, MLA_Q_RANK, MLA_HEADS * (MLA_NOPE_DIM + MLA_ROPE_DIM)),
                        MLA_Q_RANK ** -0.5),
        'w_mla_uk': nrm(ks[14], (N_A_LAYERS, MLA_KV_RANK, MLA_HEADS, MLA_NOPE_DIM), MLA_KV_RANK ** -0.5),
        'w_mla_uv': nrm(ks[15], (N_A_LAYERS, MLA_KV_RANK, MLA_HEADS, MLA_V_DIM), MLA_KV_RANK ** -0.5),
        'w_mix_in_b': nrm(ks[16], (N_B_LAYERS, D_MODEL, DIFF_IN_COLS), D_MODEL ** -0.5),
        'diff_lambda': nrm(ks[17], (N_B_LAYERS, 4, DIFF_HEAD_DIM), 0.1),
        'g_diff_sub': gain(ks[18], (N_B_LAYERS, DIFF_V_DIM)),
        'g_mem': gain(ks[19], (DEPTH, D_MODEL)),
        'w_mem_kv': nrm(ks[20], (DEPTH, D_MODEL, 2 * MEM_WIDTH), D_MODEL ** -0.5),
        'w_mix_out': nrm(ks[21], (DEPTH, MIX_WIDTH, D_MODEL), MIX_WIDTH ** -0.5),
        'w_ff_up': nrm(ks[22], (DEPTH, D_MODEL, D_FF), D_MODEL ** -0.5),
        'w_ff_down': nrm(ks[23], (DEPTH, D_FF, D_MODEL), D_FF ** -0.5),
    }


def reference(x_prompt, x_sample, mem_prompt, cache_mla, cache_diff_k, cache_diff_v,
              cache_mem_k, cache_mem_v, page_table, norm_gains, w_mix_in_a, g_mla_q,
              g_mla_kv, w_mla_uq, w_mla_uk, w_mla_uv, w_mix_in_b, diff_lambda, g_diff_sub,
              g_mem, w_mem_kv, w_mix_out, w_ff_up, w_ff_down):

    def trunk(x, pos, mem_k, mem_v, paged):
        b, s, _ = x.shape
        rows_a, rows_k, rows_v = [], [], []
        for i in range(DEPTH):
            j = i // N_MIXERS
            h = rms_norm(x, norm_gains[i, 0])
            if i % N_MIXERS == 0:
                q_nope, q_rope, rows, mem_q = mla_project(
                    h, pos, w_mix_in_a[j], g_mla_q[j], g_mla_kv[j], w_mla_uq[j])
                if paged:
                    main = mla_attend_sample(q_nope, q_rope, rows, w_mla_uk[j], w_mla_uv[j],
                                             cache_mla, j, page_table)
                else:
                    main = mla_attend_prompt(q_nope, q_rope, rows, w_mla_uk[j], w_mla_uv[j])
                rows_a.append(rows)
            else:
                q, k, v, mem_q = diff_project(h, pos, w_mix_in_b[j])
                qm, km, vm = diff_maps(q, k, v)
                if paged:
                    o = diff_attend_sample(qm, km, vm, cache_diff_k, cache_diff_v, j, page_table)
                else:
                    o = causal_attention(qm, km, vm, DIFF_SCALE)
                main = diff_combine(o, diff_lambda[j], g_diff_sub[j], lambda_init(i))
                rows_k.append(k)
                rows_v.append(v)
            mem_out = memory_attention(mem_q.reshape(b, s, MEM_HEADS, MEM_HEAD_DIM), mem_k[i], mem_v[i])
            mix = jnp.concatenate([main, mem_out.reshape(b, s, MEM_WIDTH)], axis=-1) @ w_mix_out[i]
            x = x + rms_norm(mix, norm_gains[i, 1])
            h = rms_norm(x, norm_gains[i, 2])
            ff = jnp.square(jax.nn.relu(h @ w_ff_up[i])) @ w_ff_down[i]
            x = x + rms_norm(ff, norm_gains[i, 3])
        return x, jnp.stack(rows_a), jnp.stack(rows_k), jnp.stack(rows_v)

    pos_prompt = jnp.arange(SEQ, dtype=jnp.int32)
    pos_sample = PAST_LEN + jnp.arange(DEC_SEQ, dtype=jnp.int32)
    mem_k_prompt, mem_v_prompt = memory_kv(mem_prompt, g_mem, w_mem_kv)
    y_prompt, mla_prompt, diff_k_prompt, diff_v_prompt = trunk(
        x_prompt, pos_prompt, mem_k_prompt, mem_v_prompt, False)
    y_sample, mla_sample, diff_k_sample, diff_v_sample = trunk(
        x_sample, pos_sample, cache_mem_k, cache_mem_v, True)
    return (y_prompt, y_sample, mla_prompt, mla_sample, diff_k_prompt, diff_k_sample,
            diff_v_prompt, diff_v_sample, mem_k_prompt, mem_v_prompt)
```

```python
import functools
import math

import jax
import jax.numpy as jnp
import numpy as np
from jax import lax
from jax.experimental import pallas as pl
from jax.experimental.pallas import tpu as pltpu

NORM_EPS = 1e-6
ROPE_THETA = 10000.0
NEG_INF = -1e30
N_MIXERS = 2
LANES = 128
VMEM_LIMIT_BYTES = 56 * 1024 * 1024
BF16 = jnp.bfloat16
F32 = jnp.float32
NT_DIMS = (((1,), (1,)), ((), ()))


def _params(*sem):
    return pltpu.CompilerParams(dimension_semantics=sem, vmem_limit_bytes=VMEM_LIMIT_BYTES)


def _lambda_init(layer):
    return 0.8 - 0.6 * math.exp(-0.3 * layer)


def _rms(xf, g):
    return xf * lax.rsqrt(jnp.mean(xf * xf, axis=-1, keepdims=True) + NORM_EPS) * g


def _rope128(x, cos, sin_signed):
    lane = lax.broadcasted_iota(jnp.int32, x.shape, 1)
    first_half = (lane % 64) < 32
    partner = jnp.where(first_half, pltpu.roll(x, 96, 1), pltpu.roll(x, 32, 1))
    return x * cos + partner * sin_signed


def _rope_tables(pos, half):
    inv_freq = jnp.exp(-math.log(ROPE_THETA) * jnp.arange(half, dtype=F32) / half)
    ang = pos.astype(F32)[:, None] * inv_freq[None, :]
    cos, sin = jnp.cos(ang), jnp.sin(ang)
    reps = LANES // (2 * half)
    return (jnp.tile(cos, (1, 2 * reps)),
            jnp.tile(jnp.concatenate([-sin, sin], axis=-1), (1, reps)))


def _lane_tile(x, width):
    return x if width == LANES else pltpu.repeat(x, width // LANES, axis=1)


def _online_softmax_update(s, pv_fn, m_ref, l_ref, acc_ref):
    m_prev = m_ref[...]
    m_new = jnp.maximum(m_prev, jnp.max(s, axis=1, keepdims=True))
    alpha = jnp.exp(m_prev - m_new)
    p = jnp.exp(s - _lane_tile(m_new, s.shape[1]))
    l_ref[...] = alpha * l_ref[...] + jnp.sum(p, axis=1, keepdims=True)
    acc_ref[...] = _lane_tile(alpha, acc_ref.shape[-1]) * acc_ref[...] + pv_fn(p.astype(BF16))
    m_ref[...] = m_new


def _normalized(acc_ref, l_ref):
    return acc_ref[...] / _lane_tile(l_ref[...], acc_ref.shape[-1])


def _diff_lambda(lam_ref, lam_init):
    lp = lam_ref[...]
    a = jnp.sum(lp[0:1] * lp[1:2], axis=1, keepdims=True)
    b = jnp.sum(lp[2:3] * lp[3:4], axis=1, keepdims=True)
    return jnp.exp(a) - jnp.exp(b) + lam_init


def _mem_kv_kernel(x_ref, g_ref, w_ref, k_ref, v_ref):
    h = _rms(x_ref[...], g_ref[...]).astype(BF16)
    y = jnp.dot(h, w_ref[...], preferred_element_type=F32)
    half = k_ref.shape[-1]
    k_ref[...] = y[:, :half]
    v_ref[...] = y[:, half:]


def _mem_kv(mem2d, g_mem, w_mem_kv_bf):
    depth, d, two_w = w_mem_kv_bf.shape
    rows = mem2d.shape[0]
    tm = min(rows, 512)
    out = jax.ShapeDtypeStruct((depth, rows, two_w // 2), F32)
    return pl.pallas_call(
        _mem_kv_kernel,
        grid=(depth, rows // tm),
        in_specs=[pl.BlockSpec((tm, d), lambda l, i: (i, 0)),
                  pl.BlockSpec((None, 1, d), lambda l, i: (l, 0, 0)),
                  pl.BlockSpec((None, d, two_w), lambda l, i: (l, 0, 0))],
        out_specs=[pl.BlockSpec((None, tm, two_w // 2), lambda l, i: (l, i, 0))] * 2,
        out_shape=[out, out],
        compiler_params=_params("parallel", "parallel"),
        name="mem_kv",
    )(mem2d, g_mem.reshape(depth, 1, d), w_mem_kv_bf)


def _proj_a_kernel(x_ref, g_ref, w_ref, gq_ref, gkv_ref, cos_ref, sin_ref,
                   qa_ref, rows_ref, memq_ref, *, mem_scale):
    q_rank, kv_rank, mem_w = qa_ref.shape[-1], gkv_ref.shape[-1], memq_ref.shape[-1]
    rope_d = rows_ref.shape[-1] - kv_rank
    h = _rms(x_ref[...], g_ref[...]).astype(BF16)
    y = jnp.dot(h, w_ref[...], preferred_element_type=F32)
    qa_ref[...] = _rms(y[:, :q_rank], gq_ref[...]).astype(BF16)
    rows_ref[:, :kv_rank] = _rms(y[:, q_rank:q_rank + kv_rank], gkv_ref[...])
    o = q_rank + kv_rank
    memq_ref[...] = y[:, o:o + mem_w] * mem_scale
    kr = _rope128(y[:, o + mem_w:o + mem_w + LANES], cos_ref[...], sin_ref[...])
    rows_ref[:, kv_rank:] = kr[:, :rope_d]


def _proj_a(x, g0, w_ext, g_q, g_kv, cos, sin, *, q_rank, kv_rank, rope_d, mem_w, mem_scale, tm):
    t, d = x.shape
    n_tab = cos.shape[0] // tm
    row = lambda i: (i, 0)
    const = lambda i: (0, 0)
    tab = lambda i: (i % n_tab, 0)
    return pl.pallas_call(
        functools.partial(_proj_a_kernel, mem_scale=mem_scale),
        grid=(t // tm,),
        in_specs=[pl.BlockSpec((tm, d), row),
                  pl.BlockSpec((1, d), const),
                  pl.BlockSpec(w_ext.shape, const),
                  pl.BlockSpec((1, q_rank), const),
                  pl.BlockSpec((1, kv_rank), const),
                  pl.BlockSpec((tm, LANES), tab),
                  pl.BlockSpec((tm, LANES), tab)],
        out_specs=[pl.BlockSpec((tm, q_rank), row),
                   pl.BlockSpec((tm, kv_rank + rope_d), row),
                   pl.BlockSpec((tm, mem_w), row)],
        out_shape=[jax.ShapeDtypeStruct((t, q_rank), BF16),
                   jax.ShapeDtypeStruct((t, kv_rank + rope_d), F32),
                   jax.ShapeDtypeStruct((t, mem_w), F32)],
        compiler_params=_params("parallel"),
        name="proj_a",
    )(x, g0.reshape(1, d), w_ext, g_q.reshape(1, q_rank), g_kv.reshape(1, kv_rank), cos, sin)


def _qkv_up_kernel(qa_ref, rows_ref, wq_ref, wkv_ref, cos_ref, sin_ref, q_ref, k_ref, v_ref,
                   *, nope, kv_rank, scale):
    heads = q_ref.shape[0]
    rope_d = q_ref.shape[-1] - nope
    qa = qa_ref[...]
    c = rows_ref[:, :kv_rank].astype(BF16)
    k_rope = rows_ref[:, kv_rank:].astype(BF16)
    cos, sin = cos_ref[...], sin_ref[...]
    for h in range(heads):
        r = jnp.dot(qa, wq_ref[h], preferred_element_type=F32)
        q_ref[h, :, :nope] = (r[:, :nope] * scale).astype(BF16)
        rp = _rope128(r[:, nope:nope + LANES], cos, sin)
        q_ref[h, :, nope:] = (rp[:, :rope_d] * scale).astype(BF16)
        r = jnp.dot(c, wkv_ref[h], preferred_element_type=F32)
        k_ref[h, :, :nope] = r[:, :nope].astype(BF16)
        k_ref[h, :, nope:] = k_rope
        v_ref[h] = r[:, nope:].astype(BF16)


def _qkv_up(qa, rows, w_uq_pad, w_ukv, cos, sin, *, nope, kv_rank, scale, tm):
    t, q_rank = qa.shape
    cache_d = rows.shape[1]
    heads, _, wcols = w_ukv.shape
    rope_d, v_dim = cache_d - kv_rank, wcols - nope
    n_tab = cos.shape[0] // tm
    row = lambda i: (i, 0)
    head_rows = lambda i: (0, i, 0)
    const3 = lambda i: (0, 0, 0)
    tab = lambda i: (i % n_tab, 0)
    return pl.pallas_call(
        functools.partial(_qkv_up_kernel, nope=nope, kv_rank=kv_rank, scale=scale),
        grid=(t // tm,),
        in_specs=[pl.BlockSpec((tm, q_rank), row),
                  pl.BlockSpec((tm, cache_d), row),
                  pl.BlockSpec(w_uq_pad.shape, const3),
                  pl.BlockSpec(w_ukv.shape, const3),
                  pl.BlockSpec((tm, LANES), tab),
                  pl.BlockSpec((tm, LANES), tab)],
        out_specs=[pl.BlockSpec((heads, tm, nope + rope_d), head_rows),
                   pl.BlockSpec((heads, tm, nope + rope_d), head_rows),
                   pl.BlockSpec((heads, tm, v_dim), head_rows)],
        out_shape=[jax.ShapeDtypeStruct((heads, t, nope + rope_d), BF16),
                   jax.ShapeDtypeStruct((heads, t, nope + rope_d), BF16),
                   jax.ShapeDtypeStruct((heads, t, v_dim), BF16)],
        compiler_params=_params("parallel"),
        name="qkv_up",
    )(qa, rows, w_uq_pad, w_ukv, cos, sin)


def _proj_b_kernel(x_ref, g_ref, w_ref, cos_ref, sin_ref,
                   q_ref, k_ref, v_ref, kb_ref, vb_ref, memq_ref, *, q_scale, mem_scale):
    q_cols, k_cols, v_cols = q_ref.shape[-1], k_ref.shape[-1], v_ref.shape[-1]
    h = _rms(x_ref[...], g_ref[...]).astype(BF16)
    y = jnp.dot(h, w_ref[...], preferred_element_type=F32)
    cos, sin = cos_ref[...], sin_ref[...]
    for c in range(q_cols // LANES):
        sl = slice(c * LANES, (c + 1) * LANES)
        q_ref[:, sl] = (_rope128(y[:, sl], cos, sin) * q_scale).astype(BF16)
    for c in range(k_cols // LANES):
        kr = _rope128(y[:, q_cols + c * LANES:q_cols + (c + 1) * LANES], cos, sin)
        k_ref[:, c * LANES:(c + 1) * LANES] = kr
        kb_ref[:, c * LANES:(c + 1) * LANES] = kr.astype(BF16)
    v = y[:, q_cols + k_cols:q_cols + k_cols + v_cols]
    v_ref[...] = v
    vb_ref[...] = v.astype(BF16)
    memq_ref[...] = y[:, q_cols + k_cols + v_cols:] * mem_scale


def _proj_b(x, g0, w_bf, cos, sin, *, q_cols, k_cols, v_cols, mem_w, q_scale, mem_scale, tm):
    t, d = x.shape
    n_tab = cos.shape[0] // tm
    row = lambda i: (i, 0)
    const = lambda i: (0, 0)
    tab = lambda i: (i % n_tab, 0)
    widths = [(q_cols, BF16), (k_cols, F32), (v_cols, F32), (k_cols, BF16), (v_cols, BF16), (mem_w, F32)]
    return pl.pallas_call(
        functools.partial(_proj_b_kernel, q_scale=q_scale, mem_scale=mem_scale),
        grid=(t // tm,),
        in_specs=[pl.BlockSpec((tm, d), row),
                  pl.BlockSpec((1, d), const),
                  pl.BlockSpec(w_bf.shape, const),
                  pl.BlockSpec((tm, LANES), tab),
                  pl.BlockSpec((tm, LANES), tab)],
        out_specs=[pl.BlockSpec((tm, w), row) for w, _ in widths],
        out_shape=[jax.ShapeDtypeStruct((t, w), dt) for w, dt in widths],
        compiler_params=_params("parallel"),
        name="proj_b",
    )(x, g0.reshape(1, d), w_bf, cos, sin)


def _causal_pairs(nq, tq, tk):
    qi, kj, fl = [], [], []
    for i in range(nq):
        last = (i * tq + tq - 1) // tk
        for j in range(last + 1):
            masked = (j * tk + tk - 1) > i * tq
            qi.append(i)
            kj.append(j)
            fl.append(int(masked) + 2 * int(j == last))
    return (jnp.asarray(qi, jnp.int32), jnp.asarray(kj, jnp.int32), jnp.asarray(fl, jnp.int32))


def _causal_step(q_of, k_of, v_of, n_heads, qi, kj, flags, m_ref, l_ref, acc_ref, *, tq, tk):
    def run(masked):
        for h in range(n_heads):
            s = lax.dot_general(q_of(h), k_of(h), NT_DIMS, preferred_element_type=F32)
            if masked:
                row = lax.broadcasted_iota(jnp.int32, s.shape, 0) % tq + qi * tq
                col = lax.broadcasted_iota(jnp.int32, s.shape, 1) + kj * tk
                s = jnp.where(col <= row, s, NEG_INF)
            _online_softmax_update(s, lambda p, h=h: jnp.dot(p, v_of(h), preferred_element_type=F32),
                                   m_ref.at[h], l_ref.at[h], acc_ref.at[h])

    pl.when(flags % 2 == 1)(functools.partial(run, True))
    pl.when(flags % 2 == 0)(functools.partial(run, False))


def _init_softmax(m_ref, l_ref, acc_ref):
    m_ref[...] = jnp.full(m_ref.shape, NEG_INF, F32)
    l_ref[...] = jnp.zeros(l_ref.shape, F32)
    acc_ref[...] = jnp.zeros(acc_ref.shape, F32)


def _mla_flash_kernel(qi_ref, kj_ref, fl_ref, q_ref, k_ref, v_ref, o_ref, m_ref, l_ref, acc_ref,
                      *, tq, tk):
    p = pl.program_id(2)
    qi, kj, flags = qi_ref[p], kj_ref[p], fl_ref[p]
    hb, _, dv = v_ref.shape

    @pl.when(kj == 0)
    def _init():
        _init_softmax(m_ref, l_ref, acc_ref)

    _causal_step(lambda h: q_ref[h], lambda h: k_ref[h], lambda h: v_ref[h], hb, qi, kj, flags,
                 m_ref, l_ref, acc_ref, tq=tq, tk=tk)

    @pl.when(flags >= 2)
    def _finish():
        for h in range(hb):
            o_ref[:, h * dv:(h + 1) * dv] = _normalized(acc_ref.at[h], l_ref.at[h]).astype(o_ref.dtype)


def _mla_flash(q, k, v, *, batch, tq, tk, hb):
    heads, t, dk = q.shape
    dv = v.shape[-1]
    s = t // batch
    nq, nk = s // tq, s // tk
    qi, kj, fl = _causal_pairs(nq, tq, tk)
    grid_spec = pltpu.PrefetchScalarGridSpec(
        num_scalar_prefetch=3,
        grid=(batch, heads // hb, qi.shape[0]),
        in_specs=[pl.BlockSpec((hb, tq, dk), lambda b, h, p, qi, kj, fl: (h, b * nq + qi[p], 0)),
                  pl.BlockSpec((hb, tk, dk), lambda b, h, p, qi, kj, fl: (h, b * nk + kj[p], 0)),
                  pl.BlockSpec((hb, tk, dv), lambda b, h, p, qi, kj, fl: (h, b * nk + kj[p], 0))],
        out_specs=pl.BlockSpec((tq, hb * dv), lambda b, h, p, qi, kj, fl: (b * nq + qi[p], h)),
        scratch_shapes=[pltpu.VMEM((hb, tq, LANES), F32), pltpu.VMEM((hb, tq, LANES), F32),
                        pltpu.VMEM((hb, tq, dv), F32)],
    )
    return pl.pallas_call(
        functools.partial(_mla_flash_kernel, tq=tq, tk=tk),
        grid_spec=grid_spec,
        out_shape=jax.ShapeDtypeStruct((t, heads * dv), BF16),
        compiler_params=_params("parallel", "parallel", "arbitrary"),
        name="mla_flash",
    )(qi, kj, fl, q, k, v)


def _split_maps(q, head_dim):
    lane = lax.broadcasted_iota(jnp.int32, q.shape, 1)
    zero = jnp.zeros_like(q)
    return jnp.concatenate([jnp.where(lane < head_dim, q, zero), jnp.where(lane >= head_dim, q, zero)], axis=0)


def _diff_finish(o, tq, lam, g_sub, lam_init):
    d = o[:tq] - lam * o[tq:]
    return _rms(d, g_sub) * (1.0 - lam_init)


def _diff_flash_kernel(qi_ref, kj_ref, fl_ref, q_ref, k_ref, v_ref, lam_ref, gsub_ref, o_ref,
                       qs_ref, m_ref, l_ref, acc_ref, *, tq, tk, head_dim, lam_init):
    p = pl.program_id(2)
    qi, kj, flags = qi_ref[p], kj_ref[p], fl_ref[p]

    hb, _, dk = qs_ref.shape
    dv = v_ref.shape[-1]

    @pl.when(kj == 0)
    def _init():
        _init_softmax(m_ref, l_ref, acc_ref)
        for h in range(hb):
            qs_ref[h] = _split_maps(q_ref[:, h * dk:(h + 1) * dk], head_dim)

    _causal_step(lambda h: qs_ref[h], lambda h: k_ref[...], lambda h: v_ref[...], hb, qi, kj, flags,
                 m_ref, l_ref, acc_ref, tq=tq, tk=tk)

    @pl.when(flags >= 2)
    def _finish():
        lam = _diff_lambda(lam_ref, lam_init)
        for h in range(hb):
            o = _normalized(acc_ref.at[h], l_ref.at[h])
            o_ref[:, h * dv:(h + 1) * dv] = _diff_finish(o, tq, lam, gsub_ref[...], lam_init).astype(o_ref.dtype)


def _diff_flash(q, k, v, lam_p, g_sub, *, batch, rep, lam_init, tq, tk, hb):
    t, q_cols = q.shape
    dv = g_sub.shape[-1]
    dk = dv
    heads = q_cols // dk
    assert rep % hb == 0
    s = t // batch
    nq, nk = s // tq, s // tk
    qi, kj, fl = _causal_pairs(nq, tq, tk)
    grid_spec = pltpu.PrefetchScalarGridSpec(
        num_scalar_prefetch=3,
        grid=(batch, heads // hb, qi.shape[0]),
        in_specs=[pl.BlockSpec((tq, hb * dk), lambda b, h, p, qi, kj, fl: (b * nq + qi[p], h)),
                  pl.BlockSpec((tk, dk), lambda b, h, p, qi, kj, fl: (b * nk + kj[p], h * hb // rep)),
                  pl.BlockSpec((tk, dv), lambda b, h, p, qi, kj, fl: (b * nk + kj[p], h * hb // rep)),
                  pl.BlockSpec(lam_p.shape, lambda b, h, p, qi, kj, fl: (0, 0)),
                  pl.BlockSpec((1, dv), lambda b, h, p, qi, kj, fl: (0, 0))],
        out_specs=pl.BlockSpec((tq, hb * dv), lambda b, h, p, qi, kj, fl: (b * nq + qi[p], h)),
        scratch_shapes=[pltpu.VMEM((hb, 2 * tq, dk), BF16), pltpu.VMEM((hb, 2 * tq, LANES), F32),
                        pltpu.VMEM((hb, 2 * tq, LANES), F32), pltpu.VMEM((hb, 2 * tq, dv), F32)],
    )
    return pl.pallas_call(
        functools.partial(_diff_flash_kernel, tq=tq, tk=tk, head_dim=dk // 2, lam_init=lam_init),
        grid_spec=grid_spec,
        out_shape=jax.ShapeDtypeStruct((t, heads * dv), BF16),
        compiler_params=_params("parallel", "parallel", "arbitrary"),
        name="diff_flash",
    )(qi, kj, fl, q, k, v, lam_p, g_sub.reshape(1, dv))


def _mem_attn_kernel(q_ref, k_ref, v_ref, o_ref):
    q = q_ref[...].astype(BF16)
    k = k_ref[...].astype(BF16)
    v = v_ref[...].astype(BF16)
    s = jnp.einsum('bqd,bkd->bqk', q, k, preferred_element_type=F32)
    m = jnp.max(s, axis=-1, keepdims=True)
    p = jnp.exp(s - m)
    l = jnp.sum(p, axis=-1, keepdims=True)
    o = jnp.einsum('bqk,bkd->bqd', p.astype(BF16), v, preferred_element_type=F32)
    o_ref[...] = o / l


def _mem_attn(q, mem_k, mem_v, layer, *, heads, bb, ts):
    b, s, w = q.shape
    dh = w // heads
    n_mem = mem_k.shape[2]
    kv_spec = pl.BlockSpec((None, bb, n_mem, dh), lambda i, j, h: (layer, i, 0, h))
    return pl.pallas_call(
        _mem_attn_kernel,
        grid=(b // bb, s // ts, heads),
        in_specs=[pl.BlockSpec((bb, ts, dh), lambda i, j, h: (i, j, h)), kv_spec, kv_spec],
        out_specs=pl.BlockSpec((bb, ts, dh), lambda i, j, h: (i, j, h)),
        out_shape=jax.ShapeDtypeStruct((b, s, w), F32),
        compiler_params=_params("parallel", "parallel", "parallel"),
        name="mem_attn",
    )(q, mem_k, mem_v)


def _mem_attn_cached_kernel(q_ref, k_ref, v_ref, o_ref, *, heads, n_mem):
    dh = k_ref.shape[-1]
    for h in range(heads):
        q = q_ref[:, :, h * dh:(h + 1) * dh].astype(BF16)
        k = k_ref[:, pl.ds(h, n_mem, stride=heads), :].astype(BF16)
        v = v_ref[:, pl.ds(h, n_mem, stride=heads), :].astype(BF16)
        s = jnp.einsum('bqd,bkd->bqk', q, k, preferred_element_type=F32)
        m = jnp.max(s, axis=-1, keepdims=True)
        p = jnp.exp(s - m)
        l = jnp.sum(p, axis=-1, keepdims=True)
        o = jnp.einsum('bqk,bkd->bqd', p.astype(BF16), v, preferred_element_type=F32)
        o_ref[:, :, h * dh:(h + 1) * dh] = o / l


def _mem_attn_cached(q, mem_k, mem_v, layer, *, heads, bb):
    b, s, w = q.shape
    rows, dh = mem_k.shape[2], mem_k.shape[3]
    kv_spec = pl.BlockSpec((None, bb, rows, dh), lambda i: (layer, i, 0, 0))
    return pl.pallas_call(
        functools.partial(_mem_attn_cached_kernel, heads=heads, n_mem=rows // heads),
        grid=(b // bb,),
        in_specs=[pl.BlockSpec((bb, s, w), lambda i: (i, 0, 0)), kv_spec, kv_spec],
        out_specs=pl.BlockSpec((bb, s, w), lambda i: (i, 0, 0)),
        out_shape=jax.ShapeDtypeStruct((b, s, w), F32),
        compiler_params=_params("parallel"),
        name="mem_attn_cached",
    )(q, mem_k, mem_v)


def _out_proj_kernel(main_ref, mem_ref, w1_ref, w2_ref, x_ref, g_ref, o_ref):
    mix = jnp.dot(main_ref[...].astype(BF16), w1_ref[...], preferred_element_type=F32)
    mix = mix + jnp.dot(mem_ref[...].astype(BF16), w2_ref[...], preferred_element_type=F32)
    o_ref[...] = x_ref[...] + _rms(mix, g_ref[...])


def _out_proj(main, mem_out, w_main, w_mem, x, g1, *, tm):
    t, d = x.shape
    row = lambda i: (i, 0)
    const = lambda i: (0, 0)
    return pl.pallas_call(
        _out_proj_kernel,
        grid=(t // tm,),
        in_specs=[pl.BlockSpec((tm, main.shape[1]), row),
                  pl.BlockSpec((tm, mem_out.shape[1]), row),
                  pl.BlockSpec(w_main.shape, const),
                  pl.BlockSpec(w_mem.shape, const),
                  pl.BlockSpec((tm, d), row),
                  pl.BlockSpec((1, d), const)],
        out_specs=pl.BlockSpec((tm, d), row),
        out_shape=jax.ShapeDtypeStruct((t, d), F32),
        compiler_params=_params("parallel"),
        name="out_proj",
    )(main, mem_out, w_main, w_mem, x, g1.reshape(1, d))


def _ffn_kernel(x_ref, g2_ref, g3_ref, wu_ref, wd_ref, o_ref, h_ref, acc_ref):
    j = pl.program_id(1)

    @pl.when(j == 0)
    def _init():
        h_ref[...] = _rms(x_ref[...], g2_ref[...]).astype(BF16)
        acc_ref[...] = jnp.zeros(acc_ref.shape, F32)

    u = jnp.maximum(jnp.dot(h_ref[...], wu_ref[...], preferred_element_type=F32), 0.0)
    acc_ref[...] += jnp.dot((u * u).astype(BF16), wd_ref[...], preferred_element_type=F32)

    @pl.when(j == pl.num_programs(1) - 1)
    def _finish():
        o_ref[...] = x_ref[...] + _rms(acc_ref[...], g3_ref[...])


def _ffn(x, g2, g3, w_up, w_down, *, tm, tf):
    t, d = x.shape
    f = w_up.shape[1]
    return pl.pallas_call(
        _ffn_kernel,
        grid=(t // tm, f // tf),
        in_specs=[pl.BlockSpec((tm, d), lambda i, j: (i, 0)),
                  pl.BlockSpec((1, d), lambda i, j: (0, 0)),
                  pl.BlockSpec((1, d), lambda i, j: (0, 0)),
                  pl.BlockSpec((d, tf), lambda i, j: (0, j)),
                  pl.BlockSpec((tf, d), lambda i, j: (j, 0))],
        out_specs=pl.BlockSpec((tm, d), lambda i, j: (i, 0)),
        out_shape=jax.ShapeDtypeStruct((t, d), F32),
        scratch_shapes=[pltpu.VMEM((tm, d), BF16), pltpu.VMEM((tm, d), F32)],
        compiler_params=_params("parallel", "arbitrary"),
        name="ffn",
    )(x, g2.reshape(1, d), g3.reshape(1, d), w_up, w_down)


def _q_abs_kernel(qa_ref, wq_ref, wuk_ref, cos_ref, sin_ref, o_ref, *, nope, kv_rank, scale):
    bb, t_new, width = o_ref.shape
    r = jnp.dot(qa_ref[...], wq_ref[...], preferred_element_type=F32)
    q_lat = lax.dot_general(r[:, :nope].astype(BF16), wuk_ref[...], NT_DIMS, preferred_element_type=F32)
    rp = _rope128(r[:, nope:nope + LANES], cos_ref[...], sin_ref[...])
    o_ref[:, :, :kv_rank] = (q_lat * scale).reshape(bb, t_new, kv_rank)
    o_ref[:, :, kv_rank:] = (rp[:, :width - kv_rank] * scale).reshape(bb, t_new, width - kv_rank)


def _q_abs(qa, w_uq_pad, w_uk2d, cos, sin, *, t_new, nope, kv_rank, rope_d, scale, bb):
    t, q_rank = qa.shape
    heads, _, wcols = w_uq_pad.shape
    nb = t // t_new
    tm = bb * t_new
    return pl.pallas_call(
        functools.partial(_q_abs_kernel, nope=nope, kv_rank=kv_rank, scale=scale),
        grid=(nb // bb, heads),
        in_specs=[pl.BlockSpec((tm, q_rank), lambda i, h: (i, 0)),
                  pl.BlockSpec((None, q_rank, wcols), lambda i, h: (h, 0, 0)),
                  pl.BlockSpec((kv_rank, nope), lambda i, h: (0, h)),
                  pl.BlockSpec((tm, LANES), lambda i, h: (0, 0)),
                  pl.BlockSpec((tm, LANES), lambda i, h: (0, 0))],
        out_specs=pl.BlockSpec((bb, t_new, kv_rank + rope_d), lambda i, h: (i, h, 0)),
        out_shape=jax.ShapeDtypeStruct((nb, heads * t_new, kv_rank + rope_d), F32),
        compiler_params=_params("parallel", "parallel"),
        name="q_abs",
    )(qa, w_uq_pad, w_uk2d, cos, sin)


def _new_token_scores(q, kn_ref, t_new):
    s = lax.dot_general(q, kn_ref[...], NT_DIMS, preferred_element_type=F32)
    row_t = lax.broadcasted_iota(jnp.int32, s.shape, 0) % t_new
    col = lax.broadcasted_iota(jnp.int32, s.shape, 1)
    return jnp.where(col <= row_t, s, NEG_INF)


def _paged_mla_kernel(pt_ref, q_ref, new_ref, *rest, nbb, n_pp, page, kv_rank, t_new):
    pages = rest[:nbb * n_pp]
    o_ref, qb_ref, kb_ref, kn_ref, m_ref, l_ref, acc_ref = rest[nbb * n_pp:]
    j = pl.program_id(1)

    @pl.when(j == 0)
    def _init():
        _init_softmax(m_ref, l_ref, acc_ref)
        qb_ref[...] = q_ref[...].astype(BF16)

    for i in range(nbb):
        for k in range(n_pp):
            kb_ref[i, :, k * page:(k + 1) * page] = pages[i * n_pp + k][...].astype(BF16)
        s = jnp.dot(qb_ref[i], kb_ref[i], preferred_element_type=F32)
        _online_softmax_update(
            s, lambda p, i=i: lax.dot_general(p, kb_ref[i, :kv_rank, :], NT_DIMS, preferred_element_type=F32),
            m_ref.at[i], l_ref.at[i], acc_ref.at[i])

    @pl.when(j == pl.num_programs(1) - 1)
    def _finish():
        for i in range(nbb):
            kn_ref[...] = jnp.zeros(kn_ref.shape, BF16)
            kn_ref[:t_new, :] = new_ref[i].astype(BF16)
            s_new = _new_token_scores(qb_ref[i], kn_ref, t_new)
            _online_softmax_update(
                s_new, lambda p: jnp.dot(p, kn_ref[:, :kv_rank], preferred_element_type=F32),
                m_ref.at[i], l_ref.at[i], acc_ref.at[i])
            o_ref[i] = _normalized(acc_ref.at[i], l_ref.at[i])


def _paged_mla(q_abs, rows_new, cache_t, layer, page_table, *, kv_rank, n_pp, nbb):
    nb, q_rows, width = q_abs.shape
    t_new = rows_new.shape[1]
    page = cache_t.shape[3]
    n_pages = page_table.shape[1]

    def page_spec(i, k):
        return pl.BlockSpec((None, None, width, page),
                            lambda b, j, pt: (layer, pt[b * nbb + i, j * n_pp + k], 0, 0))

    pad_rows = page
    grid_spec = pltpu.PrefetchScalarGridSpec(
        num_scalar_prefetch=1,
        grid=(nb // nbb, n_pages // n_pp),
        in_specs=[pl.BlockSpec((nbb, q_rows, width), lambda b, j, pt: (b, 0, 0)),
                  pl.BlockSpec((nbb, t_new, width), lambda b, j, pt: (b, 0, 0))]
                 + [page_spec(i, k) for i in range(nbb) for k in range(n_pp)],
        out_specs=pl.BlockSpec((nbb, q_rows, kv_rank), lambda b, j, pt: (b, 0, 0)),
        scratch_shapes=[pltpu.VMEM((nbb, q_rows, width), BF16),
                        pltpu.VMEM((nbb, width, n_pp * page), BF16),
                        pltpu.VMEM((pad_rows, width), BF16),
                        pltpu.VMEM((nbb, q_rows, LANES), F32), pltpu.VMEM((nbb, q_rows, LANES), F32),
                        pltpu.VMEM((nbb, q_rows, kv_rank), F32)],
    )
    return pl.pallas_call(
        functools.partial(_paged_mla_kernel, nbb=nbb, n_pp=n_pp, page=page, kv_rank=kv_rank, t_new=t_new),
        grid_spec=grid_spec,
        out_shape=jax.ShapeDtypeStruct((nb, q_rows, kv_rank), F32),
        compiler_params=_params("parallel", "arbitrary"),
        name="paged_mla",
    )(page_table, q_abs, rows_new, *([cache_t] * (nbb * n_pp)))


def _o_up_kernel(o_ref, w_ref, out_ref):
    bb, t_new, c = o_ref.shape
    o = o_ref[...].reshape(bb * t_new, c).astype(BF16)
    out_ref[...] = jnp.dot(o, w_ref[...], preferred_element_type=F32).astype(out_ref.dtype)


def _o_up(o_lat, w_uv2d, *, t_new, v_dim, bb):
    nb, q_rows, kv_rank = o_lat.shape
    heads = q_rows // t_new
    return pl.pallas_call(
        _o_up_kernel,
        grid=(nb // bb, heads),
        in_specs=[pl.BlockSpec((bb, t_new, kv_rank), lambda i, h: (i, h, 0)),
                  pl.BlockSpec((kv_rank, v_dim), lambda i, h: (0, h))],
        out_specs=pl.BlockSpec((bb * t_new, v_dim), lambda i, h: (i, h)),
        out_shape=jax.ShapeDtypeStruct((nb * t_new, heads * v_dim), BF16),
        compiler_params=_params("parallel", "parallel"),
        name="o_up",
    )(o_lat, w_uv2d)


def _paged_diff_kernel(pt_ref, q_ref, kn_in_ref, vn_in_ref, lam_ref, gsub_ref, *rest,
                       nbb, n_pp, page, kv_heads, t_new, lam_init):
    kpages, vpages = rest[:nbb * n_pp], rest[nbb * n_pp:2 * nbb * n_pp]
    o_ref, kb_ref, vb_ref, kn_ref, vn_ref, m_ref, l_ref, acc_ref = rest[2 * nbb * n_pp:]
    j = pl.program_id(1)
    dv = acc_ref.shape[-1]
    rows_g = q_ref.shape[1] // kv_heads

    def update(i, s, v_of_head):
        def pv_fn(p):
            return jnp.concatenate(
                [jnp.dot(p[g * rows_g:(g + 1) * rows_g], v_of_head(g), preferred_element_type=F32)
                 for g in range(kv_heads)], axis=0)
        _online_softmax_update(s, pv_fn, m_ref.at[i], l_ref.at[i], acc_ref.at[i])

    @pl.when(j == 0)
    def _init():
        _init_softmax(m_ref, l_ref, acc_ref)

    for i in range(nbb):
        for k in range(n_pp):
            kb_ref[i, :, k * page:(k + 1) * page] = kpages[i * n_pp + k][...].astype(BF16)
            for g in range(kv_heads):
                vb_ref[i, g, k * page:(k + 1) * page, :] = (
                    vpages[i * n_pp + k][pl.ds(g, page, stride=kv_heads), :].astype(BF16))
        s = jnp.dot(q_ref[i], kb_ref[i], preferred_element_type=F32)
        update(i, s, lambda g: vb_ref[i, g])

    @pl.when(j == pl.num_programs(1) - 1)
    def _finish():
        lam = _diff_lambda(lam_ref, lam_init)
        half = rows_g // 2
        for i in range(nbb):
            kn_ref[...] = jnp.zeros(kn_ref.shape, BF16)
            vn_ref[...] = jnp.zeros(vn_ref.shape, BF16)
            kn_ref[:t_new, :] = kn_in_ref[i].astype(BF16)
            vn_ref[:t_new, :] = vn_in_ref[i].astype(BF16)
            update(i, _new_token_scores(q_ref[i], kn_ref, t_new), lambda g: vn_ref[:, g * dv:(g + 1) * dv])
            o = _normalized(acc_ref.at[i], l_ref.at[i])
            for g in range(kv_heads):
                og = o[g * rows_g:(g + 1) * rows_g]
                o_ref[i, g * half:(g + 1) * half, :] = _diff_finish(og, half, lam, gsub_ref[...], lam_init)


def _paged_diff(q_bd, k_new, v_new, cache_kt, cache_v2, layer, page_table, lam_p, g_sub,
                *, kv_heads, lam_init, n_pp, nbb):
    nb, q_rows, kw = q_bd.shape
    t_new = k_new.shape[1]
    page = cache_kt.shape[3]
    n_pages = page_table.shape[1]
    dv = cache_v2.shape[3]
    vw = dv * kv_heads

    def page_spec(i, k, rows, cols):
        return pl.BlockSpec((None, None, rows, cols),
                            lambda b, j, pt: (layer, pt[b * nbb + i, j * n_pp + k], 0, 0))

    slots = [(i, k) for i in range(nbb) for k in range(n_pp)]
    pad_rows = page
    grid_spec = pltpu.PrefetchScalarGridSpec(
        num_scalar_prefetch=1,
        grid=(nb // nbb, n_pages // n_pp),
        in_specs=[pl.BlockSpec((nbb, q_rows, kw), lambda b, j, pt: (b, 0, 0)),
                  pl.BlockSpec((nbb, t_new, kw), lambda b, j, pt: (b, 0, 0)),
                  pl.BlockSpec((nbb, t_new, vw), lambda b, j, pt: (b, 0, 0)),
                  pl.BlockSpec(lam_p.shape, lambda b, j, pt: (0, 0)),
                  pl.BlockSpec((1, dv), lambda b, j, pt: (0, 0))]
                 + [page_spec(i, k, kw, page) for i, k in slots]
                 + [page_spec(i, k, page * kv_heads, dv) for i, k in slots],
        out_specs=pl.BlockSpec((nbb, q_rows // 2, dv), lambda b, j, pt: (b, 0, 0)),
        scratch_shapes=[pltpu.VMEM((nbb, kw, n_pp * page), BF16),
                        pltpu.VMEM((nbb, kv_heads, n_pp * page, dv), BF16),
                        pltpu.VMEM((pad_rows, kw), BF16), pltpu.VMEM((pad_rows, vw), BF16),
                        pltpu.VMEM((nbb, q_rows, LANES), F32), pltpu.VMEM((nbb, q_rows, LANES), F32),
                        pltpu.VMEM((nbb, q_rows, dv), F32)],
    )
    n_ops = nbb * n_pp
    return pl.pallas_call(
        functools.partial(_paged_diff_kernel, nbb=nbb, n_pp=n_pp, page=page, kv_heads=kv_heads, t_new=t_new,
                          lam_init=lam_init),
        grid_spec=grid_spec,
        out_shape=jax.ShapeDtypeStruct((nb, q_rows // 2, dv), F32),
        compiler_params=_params("parallel", "arbitrary"),
        name="paged_diff",
    )(page_table, q_bd, k_new, v_new, lam_p, g_sub.reshape(1, dv), *([cache_kt] * n_ops), *([cache_v2] * n_ops))


def _largest_divisor(n, cap):
    d = min(n, cap)
    while n % d:
        d -= 1
    return d


def kernel(x_prompt, x_sample, mem_prompt, cache_mla, cache_diff_k, cache_diff_v, cache_mem_k, cache_mem_v,
           page_table, norm_gains, w_mix_in_a, g_mla_q, g_mla_kv, w_mla_uq, w_mla_uk, w_mla_uv, w_mix_in_b,
           diff_lambda, g_diff_sub, g_mem, w_mem_kv, w_mix_out, w_ff_up, w_ff_down):
    batch, seq, d_model = x_prompt.shape
    dec_batch, dec_seq, _ = x_sample.shape
    depth = norm_gains.shape[0]
    n_mem = mem_prompt.shape[1]
    mem_heads, mem_hd = cache_mem_k.shape[3], cache_mem_k.shape[4]
    mem_w = mem_heads * mem_hd
    mem_scale = mem_hd ** -0.5
    page = cache_mla.shape[2]
    n_pages = page_table.shape[1]
    past_len = n_pages * page

    q_rank, kv_rank = g_mla_q.shape[1], g_mla_kv.shape[1]
    mla_heads, nope = w_mla_uk.shape[2], w_mla_uk.shape[3]
    mla_v = w_mla_uv.shape[3]
    rope_d = cache_mla.shape[3] - kv_rank
    mla_scale = (nope + rope_d) ** -0.5
    kv_heads, head_dim = cache_diff_k.shape[3], cache_diff_k.shape[5]
    diff_v = cache_diff_v.shape[4]
    k_cols, v_cols = kv_heads * 2 * head_dim, kv_heads * diff_v
    q_cols = w_mix_in_b.shape[2] - k_cols - v_cols - mem_w
    diff_heads = q_cols // (2 * head_dim)
    rep = diff_heads // kv_heads
    diff_scale = head_dim ** -0.5
    assert rope_d == head_dim == LANES // 2 and diff_v == 2 * head_dim == nope == mla_v == mem_hd == LANES

    cos_p, sin_p = _rope_tables(jnp.arange(seq, dtype=jnp.int32), rope_d // 2)
    cos_s, sin_s = _rope_tables(past_len + jnp.arange(dec_seq, dtype=jnp.int32), rope_d // 2)
    t_s = dec_batch * dec_seq
    tm_s = _largest_divisor(t_s, 256)
    bb_s = tm_s // dec_seq
    cos_s, sin_s = jnp.tile(cos_s, (bb_s, 1)), jnp.tile(sin_s, (bb_s, 1))

    tm_p = _largest_divisor(seq, 256)
    t_att = _largest_divisor(seq, 512)
    n_pp = _largest_divisor(n_pages, 8)
    nbb = _largest_divisor(dec_batch, 2)

    mem_k_p, mem_v_p = _mem_kv(mem_prompt.reshape(batch * n_mem, d_model), g_mem, w_mem_kv.astype(BF16))
    mem_k_p = mem_k_p.reshape(depth, batch, n_mem, mem_w)
    mem_v_p = mem_v_p.reshape(depth, batch, n_mem, mem_w)
    mem_k_s = cache_mem_k.reshape(depth, dec_batch, n_mem * mem_heads, mem_hd)
    mem_v_s = cache_mem_v.reshape(depth, dec_batch, n_mem * mem_heads, mem_hd)
    page_table = page_table.astype(jnp.int32)

    xp = x_prompt.reshape(batch * seq, d_model)
    xs = x_sample.reshape(t_s, d_model)
    rows_a_p, rows_a_s, rows_k_p, rows_k_s, rows_v_p, rows_v_s = [], [], [], [], [], []

    for i in range(depth):
        j = i // N_MIXERS
        w_out = w_mix_out[i].astype(BF16)
        w_out_main, w_out_mem = w_out[:d_model - mem_w], w_out[d_model - mem_w:]
        if i % N_MIXERS == 0:
            w_in = w_mix_in_a[j]
            o = q_rank + kv_rank
            w_ext = jnp.concatenate(
                [w_in[:, :o], w_in[:, o + rope_d:], w_in[:, o:o + rope_d],
                 jnp.zeros((d_model, LANES - rope_d), F32)], axis=1).astype(BF16)
            w_uq = w_mla_uq[j].reshape(q_rank, mla_heads, nope + rope_d)
            w_uq_pad = jnp.pad(jnp.transpose(w_uq, (1, 0, 2)),
                               ((0, 0), (0, 0), (0, nope + LANES - nope - rope_d))).astype(BF16)
            w_uk2d = w_mla_uk[j].reshape(kv_rank, mla_heads * nope).astype(BF16)
            w_uv2d = w_mla_uv[j].reshape(kv_rank, mla_heads * mla_v).astype(BF16)
            w_ukv = jnp.transpose(jnp.concatenate([w_mla_uk[j], w_mla_uv[j]], axis=-1), (1, 0, 2)).astype(BF16)
            proj = functools.partial(_proj_a, g0=norm_gains[i, 0], w_ext=w_ext, g_q=g_mla_q[j], g_kv=g_mla_kv[j],
                                     q_rank=q_rank, kv_rank=kv_rank, rope_d=rope_d, mem_w=mem_w,
                                     mem_scale=mem_scale)
            qa, rows_p, memq_p = proj(xp, cos=cos_p, sin=sin_p, tm=tm_p)
            q, k, v = _qkv_up(qa, rows_p, w_uq_pad, w_ukv, cos_p, sin_p, nope=nope, kv_rank=kv_rank,
                              scale=mla_scale, tm=tm_p)
            main_p = _mla_flash(q, k, v, batch=batch, tq=t_att, tk=t_att, hb=_largest_divisor(mla_heads, 4))
            rows_a_p.append(rows_p.reshape(batch, seq, kv_rank + rope_d))
            qa, rows_s, memq_s = proj(xs, cos=cos_s, sin=sin_s, tm=tm_s)
            q_abs = _q_abs(qa, w_uq_pad, w_uk2d, cos_s, sin_s, t_new=dec_seq, nope=nope, kv_rank=kv_rank,
                           rope_d=rope_d, scale=mla_scale, bb=bb_s)
            rows_s3 = rows_s.reshape(dec_batch, dec_seq, kv_rank + rope_d)
            o_lat = _paged_mla(q_abs, rows_s3, jnp.swapaxes(cache_mla, 2, 3), j, page_table,
                               kv_rank=kv_rank, n_pp=n_pp, nbb=nbb)
            main_s = _o_up(o_lat, w_uv2d, t_new=dec_seq, v_dim=mla_v, bb=bb_s)
            rows_a_s.append(rows_s3)
        else:
            lam_init = _lambda_init(i)
            proj = functools.partial(_proj_b, g0=norm_gains[i, 0], w_bf=w_mix_in_b[j].astype(BF16),
                                     q_cols=q_cols, k_cols=k_cols, v_cols=v_cols, mem_w=mem_w,
                                     q_scale=diff_scale, mem_scale=mem_scale)
            q, k_p, v_p, kb, vb, memq_p = proj(xp, cos=cos_p, sin=sin_p, tm=tm_p)
            main_p = _diff_flash(q, kb, vb, diff_lambda[j], g_diff_sub[j], batch=batch, rep=rep,
                                 lam_init=lam_init, tq=t_att, tk=t_att, hb=_largest_divisor(rep, 3))
            rows_k_p.append(k_p.reshape(batch, seq, kv_heads, 2, head_dim))
            rows_v_p.append(v_p.reshape(batch, seq, kv_heads, diff_v))
            q, k_s, v_s, _, _, memq_s = proj(xs, cos=cos_s, sin=sin_s, tm=tm_s)
            q6 = q.reshape(dec_batch, dec_seq, kv_heads, rep, 2, head_dim)
            q6 = jnp.transpose(q6, (0, 2, 4, 3, 1, 5)).reshape(dec_batch, 2 * kv_heads, rep * dec_seq, head_dim)
            eye = jnp.eye(2 * kv_heads, dtype=BF16)
            q_bd = (q6[:, :, :, None, :] * eye[None, :, None, :, None]).reshape(
                dec_batch, 2 * kv_heads * rep * dec_seq, k_cols)
            n_phys = cache_diff_k.shape[1]
            ck = jnp.transpose(cache_diff_k, (0, 1, 3, 4, 5, 2)).reshape(-1, n_phys, k_cols, page)
            cv = cache_diff_v.reshape(-1, n_phys, page * kv_heads, diff_v)
            o_s = _paged_diff(q_bd, k_s.reshape(dec_batch, dec_seq, k_cols), v_s.reshape(dec_batch, dec_seq, v_cols),
                              ck, cv, j, page_table, diff_lambda[j], g_diff_sub[j],
                              kv_heads=kv_heads, lam_init=lam_init, n_pp=n_pp, nbb=nbb)
            main_s = jnp.transpose(o_s.reshape(dec_batch, kv_heads, rep, dec_seq, diff_v),
                                   (0, 3, 1, 2, 4)).reshape(t_s, diff_heads * diff_v)
            rows_k_s.append(k_s.reshape(dec_batch, dec_seq, kv_heads, 2, head_dim))
            rows_v_s.append(v_s.reshape(dec_batch, dec_seq, kv_heads, diff_v))

        mem_p = _mem_attn(memq_p.reshape(batch, seq, mem_w), mem_k_p, mem_v_p, i, heads=mem_heads,
                          bb=1, ts=_largest_divisor(seq, 1024))
        mem_s = _mem_attn_cached(memq_s.reshape(dec_batch, dec_seq, mem_w), mem_k_s, mem_v_s, i, heads=mem_heads,
                                 bb=_largest_divisor(dec_batch, 8))
        w_up, w_down = w_ff_up[i].astype(BF16), w_ff_down[i].astype(BF16)
        tf = _largest_divisor(w_up.shape[1], 512)
        xp = _out_proj(main_p, mem_p.reshape(batch * seq, mem_w), w_out_main, w_out_mem, xp, norm_gains[i, 1], tm=tm_p)
        xp = _ffn(xp, norm_gains[i, 2], norm_gains[i, 3], w_up, w_down, tm=_largest_divisor(batch * seq, 512), tf=tf)
        xs = _out_proj(main_s, mem_s.reshape(t_s, mem_w), w_out_main, w_out_mem, xs, norm_gains[i, 1], tm=tm_s)
        xs = _ffn(xs, norm_gains[i, 2], norm_gains[i, 3], w_up, w_down, tm=_largest_divisor(t_s, 512), tf=tf)

    return (xp.reshape(batch, seq, d_model), xs.reshape(dec_batch, dec_seq, d_model),
            jnp.stack(rows_a_p), jnp.stack(rows_a_s), jnp.stack(rows_k_p), jnp.stack(rows_k_s),
            jnp.stack(rows_v_p), jnp.stack(rows_v_s),
            mem_k_p.reshape(depth, batch, n_mem, mem_heads, mem_hd),
            mem_v_p.reshape(depth, batch, n_mem, mem_heads, mem_hd))
```

```python
import functools
import math

import jax
import jax.numpy as jnp
import numpy as np
from jax import lax
from jax.experimental import pallas as pl
from jax.experimental.pallas import tpu as pltpu

NORM_EPS = 1e-6
ROPE_THETA = 10000.0
NEG_INF = -1e30
LOG2_E = math.log2(math.e)
N_MIXERS = 2
LANES = 128
VMEM_LIMIT_BYTES = 56 * 1024 * 1024
BF16 = jnp.bfloat16
F32 = jnp.float32
NT_DIMS = (((1,), (1,)), ((), ()))


def _params(*sem):
    return pltpu.CompilerParams(dimension_semantics=sem, vmem_limit_bytes=VMEM_LIMIT_BYTES)


def _lambda_init(layer):
    return 0.8 - 0.6 * math.exp(-0.3 * layer)


def _rms(xf, g):
    return xf * lax.rsqrt(jnp.mean(xf * xf, axis=-1, keepdims=True) + NORM_EPS) * g


def _rope128(x, cos, sin_signed):
    lane = lax.broadcasted_iota(jnp.int32, x.shape, 1)
    first_half = (lane % 64) < 32
    partner = jnp.where(first_half, pltpu.roll(x, 96, 1), pltpu.roll(x, 32, 1))
    return x * cos + partner * sin_signed


def _rope_tables(pos, half):
    inv_freq = jnp.exp(-math.log(ROPE_THETA) * jnp.arange(half, dtype=F32) / half)
    ang = pos.astype(F32)[:, None] * inv_freq[None, :]
    cos, sin = jnp.cos(ang), jnp.sin(ang)
    reps = LANES // (2 * half)
    return (jnp.tile(cos, (1, 2 * reps)),
            jnp.tile(jnp.concatenate([-sin, sin], axis=-1), (1, reps)))


def _lane_tile(x, width):
    return x if width == LANES else pltpu.repeat(x, width // LANES, axis=1)


def _online_softmax_update(s, pv_fn, m_ref, l_ref, acc_ref):
    m_prev = m_ref[...]
    m_new = jnp.maximum(m_prev, jnp.max(s, axis=1, keepdims=True))
    alpha = jnp.exp2(m_prev - m_new)
    p = jnp.exp2(s - _lane_tile(m_new, s.shape[1]))
    l_ref[...] = alpha * l_ref[...] + jnp.sum(p, axis=1, keepdims=True)
    acc_ref[...] = _lane_tile(alpha, acc_ref.shape[-1]) * acc_ref[...] + pv_fn(p.astype(BF16))
    m_ref[...] = m_new


def _normalized(acc_ref, l_ref):
    return acc_ref[...] / _lane_tile(l_ref[...], acc_ref.shape[-1])


def _diff_lambda(lam_ref, lam_init):
    lp = lam_ref[...]
    a = jnp.sum(lp[0:1] * lp[1:2], axis=1, keepdims=True)
    b = jnp.sum(lp[2:3] * lp[3:4], axis=1, keepdims=True)
    return jnp.exp(a) - jnp.exp(b) + lam_init


def _mem_kv_kernel(x_ref, g_ref, w_ref, k_ref, v_ref):
    h = _rms(x_ref[...], g_ref[...]).astype(BF16)
    y = jnp.dot(h, w_ref[...], preferred_element_type=F32)
    half = k_ref.shape[-1]
    k_ref[...] = y[:, :half]
    v_ref[...] = y[:, half:]


def _mem_kv(mem2d, g_mem, w_mem_kv_bf):
    depth, d, two_w = w_mem_kv_bf.shape
    rows = mem2d.shape[0]
    tm = min(rows, 512)
    out = jax.ShapeDtypeStruct((depth, rows, two_w // 2), F32)
    return pl.pallas_call(
        _mem_kv_kernel,
        grid=(depth, rows // tm),
        in_specs=[pl.BlockSpec((tm, d), lambda l, i: (i, 0)),
                  pl.BlockSpec((None, 1, d), lambda l, i: (l, 0, 0)),
                  pl.BlockSpec((None, d, two_w), lambda l, i: (l, 0, 0))],
        out_specs=[pl.BlockSpec((None, tm, two_w // 2), lambda l, i: (l, i, 0))] * 2,
        out_shape=[out, out],
        compiler_params=_params("parallel", "parallel"),
        name="mem_kv",
    )(mem2d, g_mem.reshape(depth, 1, d), w_mem_kv_bf)


def _proj_a_kernel(x_ref, g_ref, w_ref, gq_ref, gkv_ref, cos_ref, sin_ref,
                   qa_ref, rows_ref, memq_ref, *, mem_scale):
    q_rank, kv_rank, mem_w = qa_ref.shape[-1], gkv_ref.shape[-1], memq_ref.shape[-1]
    rope_d = rows_ref.shape[-1] - kv_rank
    h = _rms(x_ref[...], g_ref[...]).astype(BF16)
    y = jnp.dot(h, w_ref[...], preferred_element_type=F32)
    qa_ref[...] = _rms(y[:, :q_rank], gq_ref[...]).astype(BF16)
    rows_ref[:, :kv_rank] = _rms(y[:, q_rank:q_rank + kv_rank], gkv_ref[...])
    o = q_rank + kv_rank
    memq_ref[...] = y[:, o:o + mem_w] * mem_scale
    kr = _rope128(y[:, o + mem_w:o + mem_w + LANES], cos_ref[...], sin_ref[...])
    rows_ref[:, kv_rank:] = kr[:, :rope_d]


def _proj_a(x, g0, w_ext, g_q, g_kv, cos, sin, *, q_rank, kv_rank, rope_d, mem_w, mem_scale, tm):
    t, d = x.shape
    n_tab = cos.shape[0] // tm
    row = lambda i: (i, 0)
    const = lambda i: (0, 0)
    tab = lambda i: (i % n_tab, 0)
    return pl.pallas_call(
        functools.partial(_proj_a_kernel, mem_scale=mem_scale),
        grid=(t // tm,),
        in_specs=[pl.BlockSpec((tm, d), row),
                  pl.BlockSpec((1, d), const),
                  pl.BlockSpec(w_ext.shape, const),
                  pl.BlockSpec((1, q_rank), const),
                  pl.BlockSpec((1, kv_rank), const),
                  pl.BlockSpec((tm, LANES), tab),
                  pl.BlockSpec((tm, LANES), tab)],
        out_specs=[pl.BlockSpec((tm, q_rank), row),
                   pl.BlockSpec((tm, kv_rank + rope_d), row),
                   pl.BlockSpec((tm, mem_w), row)],
        out_shape=[jax.ShapeDtypeStruct((t, q_rank), BF16),
                   jax.ShapeDtypeStruct((t, kv_rank + rope_d), F32),
                   jax.ShapeDtypeStruct((t, mem_w), F32)],
        compiler_params=_params("parallel"),
        name="proj_a",
    )(x, g0.reshape(1, d), w_ext, g_q.reshape(1, q_rank), g_kv.reshape(1, kv_rank), cos, sin)


def _qkv_up_kernel(qa_ref, rows_ref, wq_ref, wkv_ref, cos_ref, sin_ref, q_ref, k_ref, v_ref,
                   *, nope, kv_rank, scale):
    heads = q_ref.shape[0]
    rope_d = q_ref.shape[-1] - nope
    qa = qa_ref[...]
    c = rows_ref[:, :kv_rank].astype(BF16)
    k_rope = rows_ref[:, kv_rank:].astype(BF16)
    cos, sin = cos_ref[...], sin_ref[...]
    for h in range(heads):
        r = jnp.dot(qa, wq_ref[h], preferred_element_type=F32)
        q_ref[h, :, :nope] = (r[:, :nope] * scale).astype(BF16)
        rp = _rope128(r[:, nope:nope + LANES], cos, sin)
        q_ref[h, :, nope:] = (rp[:, :rope_d] * scale).astype(BF16)
        r = jnp.dot(c, wkv_ref[h], preferred_element_type=F32)
        k_ref[h, :, :nope] = r[:, :nope].astype(BF16)
        k_ref[h, :, nope:] = k_rope
        v_ref[h] = r[:, nope:].astype(BF16)


def _qkv_up(qa, rows, w_uq_pad, w_ukv, cos, sin, *, nope, kv_rank, scale, tm):
    t, q_rank = qa.shape
    cache_d = rows.shape[1]
    heads, _, wcols = w_ukv.shape
    rope_d, v_dim = cache_d - kv_rank, wcols - nope
    n_tab = cos.shape[0] // tm
    row = lambda i: (i, 0)
    head_rows = lambda i: (0, i, 0)
    const3 = lambda i: (0, 0, 0)
    tab = lambda i: (i % n_tab, 0)
    return pl.pallas_call(
        functools.partial(_qkv_up_kernel, nope=nope, kv_rank=kv_rank, scale=scale),
        grid=(t // tm,),
        in_specs=[pl.BlockSpec((tm, q_rank), row),
                  pl.BlockSpec((tm, cache_d), row),
                  pl.BlockSpec(w_uq_pad.shape, const3),
                  pl.BlockSpec(w_ukv.shape, const3),
                  pl.BlockSpec((tm, LANES), tab),
                  pl.BlockSpec((tm, LANES), tab)],
        out_specs=[pl.BlockSpec((heads, tm, nope + rope_d), head_rows),
                   pl.BlockSpec((heads, tm, nope + rope_d), head_rows),
                   pl.BlockSpec((heads, tm, v_dim), head_rows)],
        out_shape=[jax.ShapeDtypeStruct((heads, t, nope + rope_d), BF16),
                   jax.ShapeDtypeStruct((heads, t, nope + rope_d), BF16),
                   jax.ShapeDtypeStruct((heads, t, v_dim), BF16)],
        compiler_params=_params("parallel"),
        name="qkv_up",
    )(qa, rows, w_uq_pad, w_ukv, cos, sin)


def _proj_b_kernel(x_ref, g_ref, w_ref, cos_ref, sin_ref,
                   q_ref, k_ref, v_ref, kb_ref, vb_ref, memq_ref, *, q_scale, mem_scale):
    q_cols, k_cols, v_cols = q_ref.shape[-1], k_ref.shape[-1], v_ref.shape[-1]
    h = _rms(x_ref[...], g_ref[...]).astype(BF16)
    y = jnp.dot(h, w_ref[...], preferred_element_type=F32)
    cos, sin = cos_ref[...], sin_ref[...]
    for c in range(q_cols // LANES):
        sl = slice(c * LANES, (c + 1) * LANES)
        q_ref[:, sl] = (_rope128(y[:, sl], cos, sin) * q_scale).astype(BF16)
    for c in range(k_cols // LANES):
        kr = _rope128(y[:, q_cols + c * LANES:q_cols + (c + 1) * LANES], cos, sin)
        k_ref[:, c * LANES:(c + 1) * LANES] = kr
        kb_ref[:, c * LANES:(c + 1) * LANES] = kr.astype(BF16)
    v = y[:, q_cols + k_cols:q_cols + k_cols + v_cols]
    v_ref[...] = v
    vb_ref[...] = v.astype(BF16)
    memq_ref[...] = y[:, q_cols + k_cols + v_cols:] * mem_scale


def _proj_b(x, g0, w_bf, cos, sin, *, q_cols, k_cols, v_cols, mem_w, q_scale, mem_scale, tm):
    t, d = x.shape
    n_tab = cos.shape[0] // tm
    row = lambda i: (i, 0)
    const = lambda i: (0, 0)
    tab = lambda i: (i % n_tab, 0)
    widths = [(q_cols, BF16), (k_cols, F32), (v_cols, F32), (k_cols, BF16), (v_cols, BF16), (mem_w, F32)]
    return pl.pallas_call(
        functools.partial(_proj_b_kernel, q_scale=q_scale, mem_scale=mem_scale),
        grid=(t // tm,),
        in_specs=[pl.BlockSpec((tm, d), row),
                  pl.BlockSpec((1, d), const),
                  pl.BlockSpec(w_bf.shape, const),
                  pl.BlockSpec((tm, LANES), tab),
                  pl.BlockSpec((tm, LANES), tab)],
        out_specs=[pl.BlockSpec((tm, w), row) for w, _ in widths],
        out_shape=[jax.ShapeDtypeStruct((t, w), dt) for w, dt in widths],
        compiler_params=_params("parallel"),
        name="proj_b",
    )(x, g0.reshape(1, d), w_bf, cos, sin)


def _causal_pairs(nq, tq, tk):
    qi, kj, fl = [], [], []
    for i in range(nq):
        last = (i * tq + tq - 1) // tk
        for j in range(last + 1):
            masked = (j * tk + tk - 1) > i * tq
            qi.append(i)
            kj.append(j)
            fl.append(int(masked) + 2 * int(j == last))
    return (jnp.asarray(qi, jnp.int32), jnp.asarray(kj, jnp.int32), jnp.asarray(fl, jnp.int32))


def _causal_step(q_of, k_of, v_of, n_heads, qi, kj, flags, m_ref, l_ref, acc_ref, *, tq, tk):
    def run(masked):
        for h in range(n_heads):
            s = lax.dot_general(q_of(h), k_of(h), NT_DIMS, preferred_element_type=F32)
            if masked:
                row = lax.broadcasted_iota(jnp.int32, s.shape, 0) % tq + qi * tq
                col = lax.broadcasted_iota(jnp.int32, s.shape, 1) + kj * tk
                s = jnp.where(col <= row, s, NEG_INF)
            _online_softmax_update(s, lambda p, h=h: jnp.dot(p, v_of(h), preferred_element_type=F32),
                                   m_ref.at[h], l_ref.at[h], acc_ref.at[h])

    pl.when(flags % 2 == 1)(functools.partial(run, True))
    pl.when(flags % 2 == 0)(functools.partial(run, False))


def _init_softmax(m_ref, l_ref, acc_ref):
    m_ref[...] = jnp.full(m_ref.shape, NEG_INF, F32)
    l_ref[...] = jnp.zeros(l_ref.shape, F32)
    acc_ref[...] = jnp.zeros(acc_ref.shape, F32)


def _mla_flash_kernel(qi_ref, kj_ref, fl_ref, q_ref, k_ref, v_ref, o_ref, m_ref, l_ref, acc_ref,
                      *, tq, tk):
    p = pl.program_id(2)
    qi, kj, flags = qi_ref[p], kj_ref[p], fl_ref[p]
    hb, _, dv = v_ref.shape

    @pl.when(kj == 0)
    def _init():
        _init_softmax(m_ref, l_ref, acc_ref)

    _causal_step(lambda h: q_ref[h], lambda h: k_ref[h], lambda h: v_ref[h], hb, qi, kj, flags,
                 m_ref, l_ref, acc_ref, tq=tq, tk=tk)

    @pl.when(flags >= 2)
    def _finish():
        for h in range(hb):
            o_ref[:, h * dv:(h + 1) * dv] = _normalized(acc_ref.at[h], l_ref.at[h]).astype(o_ref.dtype)


def _mla_flash(q, k, v, *, batch, tq, tk, hb):
    heads, t, dk = q.shape
    dv = v.shape[-1]
    s = t // batch
    nq, nk = s // tq, s // tk
    qi, kj, fl = _causal_pairs(nq, tq, tk)
    grid_spec = pltpu.PrefetchScalarGridSpec(
        num_scalar_prefetch=3,
        grid=(batch, heads // hb, qi.shape[0]),
        in_specs=[pl.BlockSpec((hb, tq, dk), lambda b, h, p, qi, kj, fl: (h, b * nq + qi[p], 0)),
                  pl.BlockSpec((hb, tk, dk), lambda b, h, p, qi, kj, fl: (h, b * nk + kj[p], 0)),
                  pl.BlockSpec((hb, tk, dv), lambda b, h, p, qi, kj, fl: (h, b * nk + kj[p], 0))],
        out_specs=pl.BlockSpec((tq, hb * dv), lambda b, h, p, qi, kj, fl: (b * nq + qi[p], h)),
        scratch_shapes=[pltpu.VMEM((hb, tq, LANES), F32), pltpu.VMEM((hb, tq, LANES), F32),
                        pltpu.VMEM((hb, tq, dv), F32)],
    )
    return pl.pallas_call(
        functools.partial(_mla_flash_kernel, tq=tq, tk=tk),
        grid_spec=grid_spec,
        out_shape=jax.ShapeDtypeStruct((t, heads * dv), BF16),
        compiler_params=_params("parallel", "parallel", "arbitrary"),
        name="mla_flash",
    )(qi, kj, fl, q, k, v)


def _split_maps(q, head_dim):
    lane = lax.broadcasted_iota(jnp.int32, q.shape, 1)
    zero = jnp.zeros_like(q)
    return jnp.concatenate([jnp.where(lane < head_dim, q, zero), jnp.where(lane >= head_dim, q, zero)], axis=0)


def _diff_finish(o, tq, lam, g_sub, lam_init):
    d = o[:tq] - lam * o[tq:]
    return _rms(d, g_sub) * (1.0 - lam_init)


def _diff_flash_kernel(qi_ref, kj_ref, fl_ref, q_ref, k_ref, v_ref, lam_ref, gsub_ref, o_ref,
                       qs_ref, m_ref, l_ref, acc_ref, *, tq, tk, head_dim, lam_init):
    p = pl.program_id(2)
    qi, kj, flags = qi_ref[p], kj_ref[p], fl_ref[p]

    hb, _, dk = qs_ref.shape
    dv = v_ref.shape[-1]

    @pl.when(kj == 0)
    def _init():
        _init_softmax(m_ref, l_ref, acc_ref)
        for h in range(hb):
            qs_ref[h] = _split_maps(q_ref[:, h * dk:(h + 1) * dk], head_dim)

    _causal_step(lambda h: qs_ref[h], lambda h: k_ref[...], lambda h: v_ref[...], hb, qi, kj, flags,
                 m_ref, l_ref, acc_ref, tq=tq, tk=tk)

    @pl.when(flags >= 2)
    def _finish():
        lam = _diff_lambda(lam_ref, lam_init)
        for h in range(hb):
            o = _normalized(acc_ref.at[h], l_ref.at[h])
            o_ref[:, h * dv:(h + 1) * dv] = _diff_finish(o, tq, lam, gsub_ref[...], lam_init).astype(o_ref.dtype)


def _diff_flash(q, k, v, lam_p, g_sub, *, batch, rep, lam_init, tq, tk, hb):
    t, q_cols = q.shape
    dv = g_sub.shape[-1]
    dk = dv
    heads = q_cols // dk
    assert rep % hb == 0
    s = t // batch
    nq, nk = s // tq, s // tk
    qi, kj, fl = _causal_pairs(nq, tq, tk)
    grid_spec = pltpu.PrefetchScalarGridSpec(
        num_scalar_prefetch=3,
        grid=(batch, heads // hb, qi.shape[0]),
        in_specs=[pl.BlockSpec((tq, hb * dk), lambda b, h, p, qi, kj, fl: (b * nq + qi[p], h)),
                  pl.BlockSpec((tk, dk), lambda b, h, p, qi, kj, fl: (b * nk + kj[p], h * hb // rep)),
                  pl.BlockSpec((tk, dv), lambda b, h, p, qi, kj, fl: (b * nk + kj[p], h * hb // rep)),
                  pl.BlockSpec(lam_p.shape, lambda b, h, p, qi, kj, fl: (0, 0)),
                  pl.BlockSpec((1, dv), lambda b, h, p, qi, kj, fl: (0, 0))],
        out_specs=pl.BlockSpec((tq, hb * dv), lambda b, h, p, qi, kj, fl: (b * nq + qi[p], h)),
        scratch_shapes=[pltpu.VMEM((hb, 2 * tq, dk), BF16), pltpu.VMEM((hb, 2 * tq, LANES), F32),
                        pltpu.VMEM((hb, 2 * tq, LANES), F32), pltpu.VMEM((hb, 2 * tq, dv), F32)],
    )
    return pl.pallas_call(
        functools.partial(_diff_flash_kernel, tq=tq, tk=tk, head_dim=dk // 2, lam_init=lam_init),
        grid_spec=grid_spec,
        out_shape=jax.ShapeDtypeStruct((t, heads * dv), BF16),
        compiler_params=_params("parallel", "parallel", "arbitrary"),
        name="diff_flash",
    )(qi, kj, fl, q, k, v, lam_p, g_sub.reshape(1, dv))


def _mem_attn_kernel(q_ref, k_ref, v_ref, o_ref):
    q = q_ref[...].astype(BF16)
    k = k_ref[...].astype(BF16)
    v = v_ref[...].astype(BF16)
    s = jnp.einsum('bqd,bkd->bqk', q, k, preferred_element_type=F32)
    m = jnp.max(s, axis=-1, keepdims=True)
    p = jnp.exp2(s - m)
    l = jnp.sum(p, axis=-1, keepdims=True)
    o = jnp.einsum('bqk,bkd->bqd', p.astype(BF16), v, preferred_element_type=F32)
    o_ref[...] = o / l


def _mem_attn(q, mem_k, mem_v, layer, *, heads, bb, ts):
    b, s, w = q.shape
    dh = w // heads
    n_mem = mem_k.shape[2]
    kv_spec = pl.BlockSpec((None, bb, n_mem, dh), lambda i, j, h: (layer, i, 0, h))
    return pl.pallas_call(
        _mem_attn_kernel,
        grid=(b // bb, s // ts, heads),
        in_specs=[pl.BlockSpec((bb, ts, dh), lambda i, j, h: (i, j, h)), kv_spec, kv_spec],
        out_specs=pl.BlockSpec((bb, ts, dh), lambda i, j, h: (i, j, h)),
        out_shape=jax.ShapeDtypeStruct((b, s, w), F32),
        compiler_params=_params("parallel", "parallel", "parallel"),
        name="mem_attn",
    )(q, mem_k, mem_v)


def _mem_attn_cached_kernel(q_ref, k_ref, v_ref, o_ref, *, heads, n_mem):
    dh = k_ref.shape[-1]
    for h in range(heads):
        q = q_ref[:, :, h * dh:(h + 1) * dh].astype(BF16)
        k = k_ref[:, pl.ds(h, n_mem, stride=heads), :].astype(BF16)
        v = v_ref[:, pl.ds(h, n_mem, stride=heads), :].astype(BF16)
        s = jnp.einsum('bqd,bkd->bqk', q, k, preferred_element_type=F32)
        m = jnp.max(s, axis=-1, keepdims=True)
        p = jnp.exp2(s - m)
        l = jnp.sum(p, axis=-1, keepdims=True)
        o = jnp.einsum('bqk,bkd->bqd', p.astype(BF16), v, preferred_element_type=F32)
        o_ref[:, :, h * dh:(h + 1) * dh] = o / l


def _mem_attn_cached(q, mem_k, mem_v, layer, *, heads, bb):
    b, s, w = q.shape
    rows, dh = mem_k.shape[2], mem_k.shape[3]
    kv_spec = pl.BlockSpec((None, bb, rows, dh), lambda i: (layer, i, 0, 0))
    return pl.pallas_call(
        functools.partial(_mem_attn_cached_kernel, heads=heads, n_mem=rows // heads),
        grid=(b // bb,),
        in_specs=[pl.BlockSpec((bb, s, w), lambda i: (i, 0, 0)), kv_spec, kv_spec],
        out_specs=pl.BlockSpec((bb, s, w), lambda i: (i, 0, 0)),
        out_shape=jax.ShapeDtypeStruct((b, s, w), F32),
        compiler_params=_params("parallel"),
        name="mem_attn_cached",
    )(q, mem_k, mem_v)


def _out_proj_kernel(main_ref, mem_ref, w1_ref, w2_ref, x_ref, g_ref, o_ref):
    mix = jnp.dot(main_ref[...].astype(BF16), w1_ref[...], preferred_element_type=F32)
    mix = mix + jnp.dot(mem_ref[...].astype(BF16), w2_ref[...], preferred_element_type=F32)
    o_ref[...] = x_ref[...] + _rms(mix, g_ref[...])


def _out_proj(main, mem_out, w_main, w_mem, x, g1, *, tm):
    t, d = x.shape
    row = lambda i: (i, 0)
    const = lambda i: (0, 0)
    return pl.pallas_call(
        _out_proj_kernel,
        grid=(t // tm,),
        in_specs=[pl.BlockSpec((tm, main.shape[1]), row),
                  pl.BlockSpec((tm, mem_out.shape[1]), row),
                  pl.BlockSpec(w_main.shape, const),
                  pl.BlockSpec(w_mem.shape, const),
                  pl.BlockSpec((tm, d), row),
                  pl.BlockSpec((1, d), const)],
        out_specs=pl.BlockSpec((tm, d), row),
        out_shape=jax.ShapeDtypeStruct((t, d), F32),
        compiler_params=_params("parallel"),
        name="out_proj",
    )(main, mem_out, w_main, w_mem, x, g1.reshape(1, d))


def _ffn_kernel(x_ref, g2_ref, g3_ref, wu_ref, wd_ref, o_ref, h_ref, acc_ref):
    j = pl.program_id(1)

    @pl.when(j == 0)
    def _init():
        h_ref[...] = _rms(x_ref[...], g2_ref[...]).astype(BF16)
        acc_ref[...] = jnp.zeros(acc_ref.shape, F32)

    u = jnp.maximum(jnp.dot(h_ref[...], wu_ref[...], preferred_element_type=F32), 0.0)
    acc_ref[...] += jnp.dot((u * u).astype(BF16), wd_ref[...], preferred_element_type=F32)

    @pl.when(j == pl.num_programs(1) - 1)
    def _finish():
        o_ref[...] = x_ref[...] + _rms(acc_ref[...], g3_ref[...])


def _ffn(x, g2, g3, w_up, w_down, layer, *, tm, tf):
    t, d = x.shape
    f = w_up.shape[2]
    return pl.pallas_call(
        _ffn_kernel,
        grid=(t // tm, f // tf),
        in_specs=[pl.BlockSpec((tm, d), lambda i, j: (i, 0), pipeline_mode=pl.Buffered(1)),
                  pl.BlockSpec((1, d), lambda i, j: (0, 0)),
                  pl.BlockSpec((1, d), lambda i, j: (0, 0)),
                  pl.BlockSpec((None, d, tf), lambda i, j: (layer, 0, j)),
                  pl.BlockSpec((None, tf, d), lambda i, j: (layer, j, 0))],
        out_specs=pl.BlockSpec((tm, d), lambda i, j: (i, 0), pipeline_mode=pl.Buffered(1)),
        out_shape=jax.ShapeDtypeStruct((t, d), F32),
        scratch_shapes=[pltpu.VMEM((tm, d), BF16), pltpu.VMEM((tm, d), F32)],
        compiler_params=_params("parallel", "arbitrary"),
        name="ffn",
    )(x, g2.reshape(1, d), g3.reshape(1, d), w_up, w_down)


def _q_abs_kernel(qa_ref, wq_ref, wuk_ref, cos_ref, sin_ref, o_ref, *, nope, kv_rank, scale):
    bb, t_new, width = o_ref.shape
    r = jnp.dot(qa_ref[...], wq_ref[...], preferred_element_type=F32)
    q_lat = lax.dot_general(r[:, :nope].astype(BF16), wuk_ref[...], NT_DIMS, preferred_element_type=F32)
    rp = _rope128(r[:, nope:nope + LANES], cos_ref[...], sin_ref[...])
    o_ref[:, :, :kv_rank] = (q_lat * scale).reshape(bb, t_new, kv_rank)
    o_ref[:, :, kv_rank:] = (rp[:, :width - kv_rank] * scale).reshape(bb, t_new, width - kv_rank)


def _q_abs(qa, w_uq_pad, w_uk2d, cos, sin, *, t_new, nope, kv_rank, rope_d, scale, bb):
    t, q_rank = qa.shape
    heads, _, wcols = w_uq_pad.shape
    nb = t // t_new
    tm = bb * t_new
    return pl.pallas_call(
        functools.partial(_q_abs_kernel, nope=nope, kv_rank=kv_rank, scale=scale),
        grid=(nb // bb, heads),
        in_specs=[pl.BlockSpec((tm, q_rank), lambda i, h: (i, 0)),
                  pl.BlockSpec((None, q_rank, wcols), lambda i, h: (h, 0, 0)),
                  pl.BlockSpec((kv_rank, nope), lambda i, h: (0, h)),
                  pl.BlockSpec((tm, LANES), lambda i, h: (0, 0)),
                  pl.BlockSpec((tm, LANES), lambda i, h: (0, 0))],
        out_specs=pl.BlockSpec((bb, t_new, kv_rank + rope_d), lambda i, h: (i, h, 0)),
        out_shape=jax.ShapeDtypeStruct((nb, heads * t_new, kv_rank + rope_d), F32),
        compiler_params=_params("parallel", "parallel"),
        name="q_abs",
    )(qa, w_uq_pad, w_uk2d, cos, sin)


def _new_token_scores(q, kn_ref, t_new):
    s = lax.dot_general(q, kn_ref[...], NT_DIMS, preferred_element_type=F32)
    row_t = lax.broadcasted_iota(jnp.int32, s.shape, 0) % t_new
    col = lax.broadcasted_iota(jnp.int32, s.shape, 1)
    return jnp.where(col <= row_t, s, NEG_INF)


def _paged_mla_kernel(pt_ref, q_ref, new_ref, *rest, nbb, n_pp, page, kv_rank, t_new):
    pages = rest[:nbb * n_pp]
    o_ref, qb_ref, kb_ref, kn_ref, m_ref, l_ref, acc_ref = rest[nbb * n_pp:]
    j = pl.program_id(1)

    @pl.when(j == 0)
    def _init():
        _init_softmax(m_ref, l_ref, acc_ref)
        qb_ref[...] = q_ref[...].astype(BF16)

    for i in range(nbb):
        for k in range(n_pp):
            kb_ref[i, :, k * page:(k + 1) * page] = pages[i * n_pp + k][...].astype(BF16)
        s = jnp.dot(qb_ref[i], kb_ref[i], preferred_element_type=F32)
        _online_softmax_update(
            s, lambda p, i=i: lax.dot_general(p, kb_ref[i, :kv_rank, :], NT_DIMS, preferred_element_type=F32),
            m_ref.at[i], l_ref.at[i], acc_ref.at[i])

    @pl.when(j == pl.num_programs(1) - 1)
    def _finish():
        for i in range(nbb):
            kn_ref[...] = jnp.zeros(kn_ref.shape, BF16)
            kn_ref[:t_new, :] = new_ref[i].astype(BF16)
            s_new = _new_token_scores(qb_ref[i], kn_ref, t_new)
            _online_softmax_update(
                s_new, lambda p: jnp.dot(p, kn_ref[:, :kv_rank], preferred_element_type=F32),
                m_ref.at[i], l_ref.at[i], acc_ref.at[i])
            o_ref[i] = _normalized(acc_ref.at[i], l_ref.at[i])


def _paged_mla(q_abs, rows_new, cache_t, layer, page_table, *, kv_rank, n_pp, nbb):
    nb, q_rows, width = q_abs.shape
    t_new = rows_new.shape[1]
    page = cache_t.shape[3]
    n_pages = page_table.shape[1]

    def page_spec(i, k):
        return pl.BlockSpec((None, None, width, page),
                            lambda b, j, pt: (layer, pt[b * nbb + i, j * n_pp + k], 0, 0))

    pad_rows = page
    grid_spec = pltpu.PrefetchScalarGridSpec(
        num_scalar_prefetch=1,
        grid=(nb // nbb, n_pages // n_pp),
        in_specs=[pl.BlockSpec((nbb, q_rows, width), lambda b, j, pt: (b, 0, 0)),
                  pl.BlockSpec((nbb, t_new, width), lambda b, j, pt: (b, 0, 0))]
                 + [page_spec(i, k) for i in range(nbb) for k in range(n_pp)],
        out_specs=pl.BlockSpec((nbb, q_rows, kv_rank), lambda b, j, pt: (b, 0, 0)),
        scratch_shapes=[pltpu.VMEM((nbb, q_rows, width), BF16),
                        pltpu.VMEM((nbb, width, n_pp * page), BF16),
                        pltpu.VMEM((pad_rows, width), BF16),
                        pltpu.VMEM((nbb, q_rows, LANES), F32), pltpu.VMEM((nbb, q_rows, LANES), F32),
                        pltpu.VMEM((nbb, q_rows, kv_rank), F32)],
    )
    return pl.pallas_call(
        functools.partial(_paged_mla_kernel, nbb=nbb, n_pp=n_pp, page=page, kv_rank=kv_rank, t_new=t_new),
        grid_spec=grid_spec,
        out_shape=jax.ShapeDtypeStruct((nb, q_rows, kv_rank), F32),
        compiler_params=_params("parallel", "arbitrary"),
        name="paged_mla",
    )(page_table, q_abs, rows_new, *([cache_t] * (nbb * n_pp)))


def _o_up_kernel(o_ref, w_ref, out_ref):
    bb, t_new, c = o_ref.shape
    o = o_ref[...].reshape(bb * t_new, c).astype(BF16)
    out_ref[...] = jnp.dot(o, w_ref[...], preferred_element_type=F32).astype(out_ref.dtype)


def _o_up(o_lat, w_uv2d, *, t_new, v_dim, bb):
    nb, q_rows, kv_rank = o_lat.shape
    heads = q_rows // t_new
    return pl.pallas_call(
        _o_up_kernel,
        grid=(nb // bb, heads),
        in_specs=[pl.BlockSpec((bb, t_new, kv_rank), lambda i, h: (i, h, 0)),
                  pl.BlockSpec((kv_rank, v_dim), lambda i, h: (0, h))],
        out_specs=pl.BlockSpec((bb * t_new, v_dim), lambda i, h: (i, h)),
        out_shape=jax.ShapeDtypeStruct((nb * t_new, heads * v_dim), BF16),
        compiler_params=_params("parallel", "parallel"),
        name="o_up",
    )(o_lat, w_uv2d)


def _paged_diff_kernel(pt_ref, q_ref, kn_in_ref, vn_in_ref, lam_ref, gsub_ref, *rest,
                       nbb, n_pp, page, kv_heads, t_new, lam_init):
    kpages, vpages = rest[:nbb * n_pp], rest[nbb * n_pp:2 * nbb * n_pp]
    o_ref, kb_ref, vb_ref, kn_ref, vn_ref, m_ref, l_ref, acc_ref = rest[2 * nbb * n_pp:]
    j = pl.program_id(1)
    dv = acc_ref.shape[-1]
    rows_g = q_ref.shape[1] // kv_heads

    def update(i, s, v_of_head):
        def pv_fn(p):
            return jnp.concatenate(
                [jnp.dot(p[g * rows_g:(g + 1) * rows_g], v_of_head(g), preferred_element_type=F32)
                 for g in range(kv_heads)], axis=0)
        _online_softmax_update(s, pv_fn, m_ref.at[i], l_ref.at[i], acc_ref.at[i])

    @pl.when(j == 0)
    def _init():
        _init_softmax(m_ref, l_ref, acc_ref)

    for i in range(nbb):
        for k in range(n_pp):
            kb_ref[i, :, k * page:(k + 1) * page] = kpages[i * n_pp + k][...].astype(BF16)
            for g in range(kv_heads):
                vb_ref[i, g, k * page:(k + 1) * page, :] = (
                    vpages[i * n_pp + k][pl.ds(g, page, stride=kv_heads), :].astype(BF16))
        s = jnp.dot(q_ref[i], kb_ref[i], preferred_element_type=F32)
        update(i, s, lambda g: vb_ref[i, g])

    @pl.when(j == pl.num_programs(1) - 1)
    def _finish():
        lam = _diff_lambda(lam_ref, lam_init)
        half = rows_g // 2
        for i in range(nbb):
            kn_ref[...] = jnp.zeros(kn_ref.shape, BF16)
            vn_ref[...] = jnp.zeros(vn_ref.shape, BF16)
            kn_ref[:t_new, :] = kn_in_ref[i].astype(BF16)
            vn_ref[:t_new, :] = vn_in_ref[i].astype(BF16)
            update(i, _new_token_scores(q_ref[i], kn_ref, t_new), lambda g: vn_ref[:, g * dv:(g + 1) * dv])
            o = _normalized(acc_ref.at[i], l_ref.at[i])
            for g in range(kv_heads):
                og = o[g * rows_g:(g + 1) * rows_g]
                o_ref[i, g * half:(g + 1) * half, :] = _diff_finish(og, half, lam, gsub_ref[...], lam_init)


def _paged_diff(q_bd, k_new, v_new, cache_kt, cache_v2, layer, page_table, lam_p, g_sub,
                *, kv_heads, lam_init, n_pp, nbb):
    nb, q_rows, kw = q_bd.shape
    t_new = k_new.shape[1]
    page = cache_kt.shape[3]
    n_pages = page_table.shape[1]
    dv = cache_v2.shape[3]
    vw = dv * kv_heads

    def page_spec(i, k, rows, cols):
        return pl.BlockSpec((None, None, rows, cols),
                            lambda b, j, pt: (layer, pt[b * nbb + i, j * n_pp + k], 0, 0))

    slots = [(i, k) for i in range(nbb) for k in range(n_pp)]
    pad_rows = page
    grid_spec = pltpu.PrefetchScalarGridSpec(
        num_scalar_prefetch=1,
        grid=(nb // nbb, n_pages // n_pp),
        in_specs=[pl.BlockSpec((nbb, q_rows, kw), lambda b, j, pt: (b, 0, 0)),
                  pl.BlockSpec((nbb, t_new, kw), lambda b, j, pt: (b, 0, 0)),
                  pl.BlockSpec((nbb, t_new, vw), lambda b, j, pt: (b, 0, 0)),
                  pl.BlockSpec(lam_p.shape, lambda b, j, pt: (0, 0)),
                  pl.BlockSpec((1, dv), lambda b, j, pt: (0, 0))]
                 + [page_spec(i, k, kw, page) for i, k in slots]
                 + [page_spec(i, k, page * kv_heads, dv) for i, k in slots],
        out_specs=pl.BlockSpec((nbb, q_rows // 2, dv), lambda b, j, pt: (b, 0, 0)),
        scratch_shapes=[pltpu.VMEM((nbb, kw, n_pp * page), BF16),
                        pltpu.VMEM((nbb, kv_heads, n_pp * page, dv), BF16),
                        pltpu.VMEM((pad_rows, kw), BF16), pltpu.VMEM((pad_rows, vw), BF16),
                        pltpu.VMEM((nbb, q_rows, LANES), F32), pltpu.VMEM((nbb, q_rows, LANES), F32),
                        pltpu.VMEM((nbb, q_rows, dv), F32)],
    )
    n_ops = nbb * n_pp
    return pl.pallas_call(
        functools.partial(_paged_diff_kernel, nbb=nbb, n_pp=n_pp, page=page, kv_heads=kv_heads, t_new=t_new,
                          lam_init=lam_init),
        grid_spec=grid_spec,
        out_shape=jax.ShapeDtypeStruct((nb, q_rows // 2, dv), F32),
        compiler_params=_params("parallel", "arbitrary"),
        name="paged_diff",
    )(page_table, q_bd, k_new, v_new, lam_p, g_sub.reshape(1, dv), *([cache_kt] * n_ops), *([cache_v2] * n_ops))


def _largest_divisor(n, cap):
    d = min(n, cap)
    while n % d:
        d -= 1
    return d


def kernel(x_prompt, x_sample, mem_prompt, cache_mla, cache_diff_k, cache_diff_v, cache_mem_k, cache_mem_v,
           page_table, norm_gains, w_mix_in_a, g_mla_q, g_mla_kv, w_mla_uq, w_mla_uk, w_mla_uv, w_mix_in_b,
           diff_lambda, g_diff_sub, g_mem, w_mem_kv, w_mix_out, w_ff_up, w_ff_down):
    batch, seq, d_model = x_prompt.shape
    dec_batch, dec_seq, _ = x_sample.shape
    depth = norm_gains.shape[0]
    n_mem = mem_prompt.shape[1]
    mem_heads, mem_hd = cache_mem_k.shape[3], cache_mem_k.shape[4]
    mem_w = mem_heads * mem_hd
    mem_scale = LOG2_E * mem_hd ** -0.5
    page = cache_mla.shape[2]
    n_pages = page_table.shape[1]
    past_len = n_pages * page

    q_rank, kv_rank = g_mla_q.shape[1], g_mla_kv.shape[1]
    mla_heads, nope = w_mla_uk.shape[2], w_mla_uk.shape[3]
    mla_v = w_mla_uv.shape[3]
    rope_d = cache_mla.shape[3] - kv_rank
    mla_scale = LOG2_E * (nope + rope_d) ** -0.5
    kv_heads, head_dim = cache_diff_k.shape[3], cache_diff_k.shape[5]
    diff_v = cache_diff_v.shape[4]
    k_cols, v_cols = kv_heads * 2 * head_dim, kv_heads * diff_v
    q_cols = w_mix_in_b.shape[2] - k_cols - v_cols - mem_w
    diff_heads = q_cols // (2 * head_dim)
    rep = diff_heads // kv_heads
    diff_scale = LOG2_E * head_dim ** -0.5
    assert rope_d == head_dim == LANES // 2 and diff_v == 2 * head_dim == nope == mla_v == mem_hd == LANES

    cos_p, sin_p = _rope_tables(jnp.arange(seq, dtype=jnp.int32), rope_d // 2)
    cos_s, sin_s = _rope_tables(past_len + jnp.arange(dec_seq, dtype=jnp.int32), rope_d // 2)
    t_s = dec_batch * dec_seq
    tm_s = _largest_divisor(t_s, 256)
    bb_s = tm_s // dec_seq
    cos_s, sin_s = jnp.tile(cos_s, (bb_s, 1)), jnp.tile(sin_s, (bb_s, 1))

    tm_p = _largest_divisor(seq, 256)
    t_att = _largest_divisor(seq, 512)
    n_pp = _largest_divisor(n_pages, 16)
    nbb = _largest_divisor(dec_batch, 2)

    mem_k_p, mem_v_p = _mem_kv(mem_prompt.reshape(batch * n_mem, d_model), g_mem, w_mem_kv.astype(BF16))
    mem_k_p = mem_k_p.reshape(depth, batch, n_mem, mem_w)
    mem_v_p = mem_v_p.reshape(depth, batch, n_mem, mem_w)
    mem_k_s = cache_mem_k.reshape(depth, dec_batch, n_mem * mem_heads, mem_hd)
    mem_v_s = cache_mem_v.reshape(depth, dec_batch, n_mem * mem_heads, mem_hd)
    page_table = page_table.astype(jnp.int32)
    w_up_bf, w_down_bf = w_ff_up.astype(BF16), w_ff_down.astype(BF16)
    tf = _largest_divisor(w_ff_up.shape[2], 512)

    xp = x_prompt.reshape(batch * seq, d_model)
    xs = x_sample.reshape(t_s, d_model)
    rows_a_p, rows_a_s, rows_k_p, rows_k_s, rows_v_p, rows_v_s = [], [], [], [], [], []

    for i in range(depth):
        j = i // N_MIXERS
        w_out = w_mix_out[i].astype(BF16)
        w_out_main, w_out_mem = w_out[:d_model - mem_w], w_out[d_model - mem_w:]
        if i % N_MIXERS == 0:
            w_in = w_mix_in_a[j]
            o = q_rank + kv_rank
            w_ext = jnp.concatenate(
                [w_in[:, :o], w_in[:, o + rope_d:], w_in[:, o:o + rope_d],
                 jnp.zeros((d_model, LANES - rope_d), F32)], axis=1).astype(BF16)
            w_uq = w_mla_uq[j].reshape(q_rank, mla_heads, nope + rope_d)
            w_uq_pad = jnp.pad(jnp.transpose(w_uq, (1, 0, 2)),
                               ((0, 0), (0, 0), (0, nope + LANES - nope - rope_d))).astype(BF16)
            w_uk2d = w_mla_uk[j].reshape(kv_rank, mla_heads * nope).astype(BF16)
            w_uv2d = w_mla_uv[j].reshape(kv_rank, mla_heads * mla_v).astype(BF16)
            w_ukv = jnp.transpose(jnp.concatenate([w_mla_uk[j], w_mla_uv[j]], axis=-1), (1, 0, 2)).astype(BF16)
            proj = functools.partial(_proj_a, g0=norm_gains[i, 0], w_ext=w_ext, g_q=g_mla_q[j], g_kv=g_mla_kv[j],
                                     q_rank=q_rank, kv_rank=kv_rank, rope_d=rope_d, mem_w=mem_w,
                                     mem_scale=mem_scale)
            qa, rows_p, memq_p = proj(xp, cos=cos_p, sin=sin_p, tm=tm_p)
            q, k, v = _qkv_up(qa, rows_p, w_uq_pad, w_ukv, cos_p, sin_p, nope=nope, kv_rank=kv_rank,
                              scale=mla_scale, tm=tm_p)
            main_p = _mla_flash(q, k, v, batch=batch, tq=t_att, tk=t_att, hb=_largest_divisor(mla_heads, 6))
            rows_a_p.append(rows_p.reshape(batch, seq, kv_rank + rope_d))
            qa, rows_s, memq_s = proj(xs, cos=cos_s, sin=sin_s, tm=tm_s)
            q_abs = _q_abs(qa, w_uq_pad, w_uk2d, cos_s, sin_s, t_new=dec_seq, nope=nope, kv_rank=kv_rank,
                           rope_d=rope_d, scale=mla_scale, bb=bb_s)
            rows_s3 = rows_s.reshape(dec_batch, dec_seq, kv_rank + rope_d)
            o_lat = _paged_mla(q_abs, rows_s3, jnp.swapaxes(cache_mla, 2, 3), j, page_table,
                               kv_rank=kv_rank, n_pp=n_pp, nbb=nbb)
            main_s = _o_up(o_lat, w_uv2d, t_new=dec_seq, v_dim=mla_v, bb=bb_s)
            rows_a_s.append(rows_s3)
        else:
            lam_init = _lambda_init(i)
            proj = functools.partial(_proj_b, g0=norm_gains[i, 0], w_bf=w_mix_in_b[j].astype(BF16),
                                     q_cols=q_cols, k_cols=k_cols, v_cols=v_cols, mem_w=mem_w,
                                     q_scale=diff_scale, mem_scale=mem_scale)
            q, k_p, v_p, kb, vb, memq_p = proj(xp, cos=cos_p, sin=sin_p, tm=tm_p)
            main_p = _diff_flash(q, kb, vb, diff_lambda[j], g_diff_sub[j], batch=batch, rep=rep,
                                 lam_init=lam_init, tq=t_att, tk=t_att, hb=_largest_divisor(rep, 6))
            rows_k_p.append(k_p.reshape(batch, seq, kv_heads, 2, head_dim))
            rows_v_p.append(v_p.reshape(batch, seq, kv_heads, diff_v))
            q, k_s, v_s, _, _, memq_s = proj(xs, cos=cos_s, sin=sin_s, tm=tm_s)
            q6 = q.reshape(dec_batch, dec_seq, kv_heads, rep, 2, head_dim)
            q6 = jnp.transpose(q6, (0, 2, 4, 3, 1, 5)).reshape(dec_batch, 2 * kv_heads, rep * dec_seq, head_dim)
            eye = jnp.eye(2 * kv_heads, dtype=BF16)
            q_bd = (q6[:, :, :, None, :] * eye[None, :, None, :, None]).reshape(
                dec_batch, 2 * kv_heads * rep * dec_seq, k_cols)
            n_phys = cache_diff_k.shape[1]
            ck = jnp.transpose(cache_diff_k, (0, 1, 3, 4, 5, 2)).reshape(-1, n_phys, k_cols, page)
            cv = cache_diff_v.reshape(-1, n_phys, page * kv_heads, diff_v)
            o_s = _paged_diff(q_bd, k_s.reshape(dec_batch, dec_seq, k_cols), v_s.reshape(dec_batch, dec_seq, v_cols),
                              ck, cv, j, page_table, diff_lambda[j], g_diff_sub[j],
                              kv_heads=kv_heads, lam_init=lam_init, n_pp=n_pp, nbb=nbb)
            main_s = jnp.transpose(o_s.reshape(dec_batch, kv_heads, rep, dec_seq, diff_v),
                                   (0, 3, 1, 2, 4)).reshape(t_s, diff_heads * diff_v)
            rows_k_s.append(k_s.reshape(dec_batch, dec_seq, kv_heads, 2, head_dim))
            rows_v_s.append(v_s.reshape(dec_batch, dec_seq, kv_heads, diff_v))

        mem_p = _mem_attn(memq_p.reshape(batch, seq, mem_w), mem_k_p, mem_v_p, i, heads=mem_heads,
                          bb=1, ts=_largest_divisor(seq, 1024))
        mem_s = _mem_attn_cached(memq_s.reshape(dec_batch, dec_seq, mem_w), mem_k_s, mem_v_s, i, heads=mem_heads,
                                 bb=_largest_divisor(dec_batch, 8))
        xp = _out_proj(main_p, mem_p.reshape(batch * seq, mem_w), w_out_main, w_out_mem, xp, norm_gains[i, 1], tm=tm_p)
        xp = _ffn(xp, norm_gains[i, 2], norm_gains[i, 3], w_up_bf, w_down_bf, i,
                  tm=_largest_divisor(batch * seq, 1024), tf=tf)
        xs = _out_proj(main_s, mem_s.reshape(t_s, mem_w), w_out_main, w_out_mem, xs, norm_gains[i, 1], tm=tm_s)
        xs = _ffn(xs, norm_gains[i, 2], norm_gains[i, 3], w_up_bf, w_down_bf, i,
                  tm=_largest_divisor(t_s, 512), tf=tf)

    return (xp.reshape(batch, seq, d_model), xs.reshape(dec_batch, dec_seq, d_model),
            jnp.stack(rows_a_p), jnp.stack(rows_a_s), jnp.stack(rows_k_p), jnp.stack(rows_k_s),
            jnp.stack(rows_v_p), jnp.stack(rows_v_s),
            mem_k_p.reshape(depth, batch, n_mem, mem_heads, mem_hd),
            mem_v_p.reshape(depth, batch, n_mem, mem_heads, mem_hd))
```

```python
import functools
import math

import jax
import jax.numpy as jnp
from jax import lax
from jax.experimental import pallas as pl
from jax.experimental.pallas import tpu as pltpu

NORM_EPS = 1e-6
ROPE_THETA = 10000.0
NEG_INF = -1e30
LOG2_E = math.log2(math.e)
N_MIXERS = 2
LANES = 128
VMEM_LIMIT_BYTES = 56 * 1024 * 1024
BF16 = jnp.bfloat16
F32 = jnp.float32
NT_DIMS = (((1,), (1,)), ((), ()))


def _params(*sem):
    return pltpu.CompilerParams(dimension_semantics=sem, vmem_limit_bytes=VMEM_LIMIT_BYTES)


def _lambda_init(layer):
    return 0.8 - 0.6 * math.exp(-0.3 * layer)


def _rms(xf, g):
    return xf * lax.rsqrt(jnp.mean(xf * xf, axis=-1, keepdims=True) + NORM_EPS) * g


def _rope128(x, cos, sin_signed):
    lane = lax.broadcasted_iota(jnp.int32, x.shape, 1)
    first_half = (lane % 64) < 32
    partner = jnp.where(first_half, pltpu.roll(x, 96, 1), pltpu.roll(x, 32, 1))
    return x * cos + partner * sin_signed


def _rope_tables(pos, half):
    inv_freq = jnp.exp(-math.log(ROPE_THETA) * jnp.arange(half, dtype=F32) / half)
    ang = pos.astype(F32)[:, None] * inv_freq[None, :]
    cos, sin = jnp.cos(ang), jnp.sin(ang)
    reps = LANES // (2 * half)
    return (jnp.tile(cos, (1, 2 * reps)),
            jnp.tile(jnp.concatenate([-sin, sin], axis=-1), (1, reps)))


def _lane_tile(x, width):
    return x if width == LANES else pltpu.repeat(x, width // LANES, axis=1)


def _online_softmax_update(s, pv_fn, m_ref, l_ref, acc_ref):
    m_prev = m_ref[...]
    m_new = jnp.maximum(m_prev, jnp.max(s, axis=1, keepdims=True))
    alpha = jnp.exp2(m_prev - m_new)
    p = jnp.exp2(s - _lane_tile(m_new, s.shape[1]))
    l_ref[...] = alpha * l_ref[...] + jnp.sum(p, axis=1, keepdims=True)
    acc_ref[...] = _lane_tile(alpha, acc_ref.shape[-1]) * acc_ref[...] + pv_fn(p.astype(BF16))
    m_ref[...] = m_new


def _normalized(acc_ref, l_ref):
    return acc_ref[...] / _lane_tile(l_ref[...], acc_ref.shape[-1])


def _diff_lambda(lam_ref, lam_init):
    lp = lam_ref[...]
    a = jnp.sum(lp[0:1] * lp[1:2], axis=1, keepdims=True)
    b = jnp.sum(lp[2:3] * lp[3:4], axis=1, keepdims=True)
    return jnp.exp(a) - jnp.exp(b) + lam_init


def _mem_kv_kernel(x_ref, g_ref, w_ref, k_ref, v_ref):
    h = _rms(x_ref[...], g_ref[...]).astype(BF16)
    y = jnp.dot(h, w_ref[...], preferred_element_type=F32)
    half = k_ref.shape[-1]
    k_ref[...] = y[:, :half]
    v_ref[...] = y[:, half:]


def _mem_kv(mem2d, g_mem, w_mem_kv_bf):
    depth, d, two_w = w_mem_kv_bf.shape
    rows = mem2d.shape[0]
    tm = min(rows, 512)
    out = jax.ShapeDtypeStruct((depth, rows, two_w // 2), F32)
    return pl.pallas_call(
        _mem_kv_kernel,
        grid=(depth, rows // tm),
        in_specs=[pl.BlockSpec((tm, d), lambda l, i: (i, 0)),
                  pl.BlockSpec((None, 1, d), lambda l, i: (l, 0, 0)),
                  pl.BlockSpec((None, d, two_w), lambda l, i: (l, 0, 0))],
        out_specs=[pl.BlockSpec((None, tm, two_w // 2), lambda l, i: (l, i, 0))] * 2,
        out_shape=[out, out],
        compiler_params=_params("parallel", "parallel"),
        name="mem_kv",
    )(mem2d, g_mem.reshape(depth, 1, d), w_mem_kv_bf)


def _proj_a_kernel(x_ref, g_ref, w_ref, gq_ref, gkv_ref, cos_ref, sin_ref,
                   qa_ref, rows_ref, memq_ref, *, mem_scale):
    q_rank, kv_rank, mem_w = qa_ref.shape[-1], gkv_ref.shape[-1], memq_ref.shape[-1]
    rope_d = rows_ref.shape[-1] - kv_rank
    h = _rms(x_ref[...], g_ref[...]).astype(BF16)
    y = jnp.dot(h, w_ref[...], preferred_element_type=F32)
    qa_ref[...] = _rms(y[:, :q_rank], gq_ref[...]).astype(BF16)
    rows_ref[:, :kv_rank] = _rms(y[:, q_rank:q_rank + kv_rank], gkv_ref[...])
    o = q_rank + kv_rank
    memq_ref[...] = y[:, o:o + mem_w] * mem_scale
    kr = _rope128(y[:, o + mem_w:o + mem_w + LANES], cos_ref[...], sin_ref[...])
    rows_ref[:, kv_rank:] = kr[:, :rope_d]


def _proj_a(x, g0, w_ext, g_q, g_kv, cos, sin, *, q_rank, kv_rank, rope_d, mem_w, mem_scale, tm):
    t, d = x.shape
    n_tab = cos.shape[0] // tm
    row = lambda i: (i, 0)
    const = lambda i: (0, 0)
    tab = lambda i: (i % n_tab, 0)
    return pl.pallas_call(
        functools.partial(_proj_a_kernel, mem_scale=mem_scale),
        grid=(t // tm,),
        in_specs=[pl.BlockSpec((tm, d), row),
                  pl.BlockSpec((1, d), const),
                  pl.BlockSpec(w_ext.shape, const),
                  pl.BlockSpec((1, q_rank), const),
                  pl.BlockSpec((1, kv_rank), const),
                  pl.BlockSpec((tm, LANES), tab),
                  pl.BlockSpec((tm, LANES), tab)],
        out_specs=[pl.BlockSpec((tm, q_rank), row),
                   pl.BlockSpec((tm, kv_rank + rope_d), row),
                   pl.BlockSpec((tm, mem_w), row)],
        out_shape=[jax.ShapeDtypeStruct((t, q_rank), BF16),
                   jax.ShapeDtypeStruct((t, kv_rank + rope_d), F32),
                   jax.ShapeDtypeStruct((t, mem_w), F32)],
        compiler_params=_params("parallel"),
        name="proj_a",
    )(x, g0.reshape(1, d), w_ext, g_q.reshape(1, q_rank), g_kv.reshape(1, kv_rank), cos, sin)


def _qkv_up_kernel(qa_ref, rows_ref, wq_ref, wkv_ref, cos_ref, sin_ref, q_ref, k_ref, v_ref,
                   *, nope, kv_rank, scale):
    heads = q_ref.shape[0]
    rope_d = q_ref.shape[-1] - nope
    qa = qa_ref[...]
    c = rows_ref[:, :kv_rank].astype(BF16)
    k_rope = rows_ref[:, kv_rank:].astype(BF16)
    cos, sin = cos_ref[...], sin_ref[...]
    for h in range(heads):
        r = jnp.dot(qa, wq_ref[h], preferred_element_type=F32)
        q_ref[h, :, :nope] = (r[:, :nope] * scale).astype(BF16)
        rp = _rope128(r[:, nope:nope + LANES], cos, sin)
        q_ref[h, :, nope:] = (rp[:, :rope_d] * scale).astype(BF16)
        r = jnp.dot(c, wkv_ref[h], preferred_element_type=F32)
        k_ref[h, :, :nope] = r[:, :nope].astype(BF16)
        k_ref[h, :, nope:] = k_rope
        v_ref[h] = r[:, nope:].astype(BF16)


def _qkv_up(qa, rows, w_uq_pad, w_ukv, cos, sin, *, nope, kv_rank, scale, tm):
    t, q_rank = qa.shape
    cache_d = rows.shape[1]
    heads, _, wcols = w_ukv.shape
    rope_d, v_dim = cache_d - kv_rank, wcols - nope
    n_tab = cos.shape[0] // tm
    row = lambda i: (i, 0)
    head_rows = lambda i: (0, i, 0)
    const3 = lambda i: (0, 0, 0)
    tab = lambda i: (i % n_tab, 0)
    return pl.pallas_call(
        functools.partial(_qkv_up_kernel, nope=nope, kv_rank=kv_rank, scale=scale),
        grid=(t // tm,),
        in_specs=[pl.BlockSpec((tm, q_rank), row),
                  pl.BlockSpec((tm, cache_d), row),
                  pl.BlockSpec(w_uq_pad.shape, const3),
                  pl.BlockSpec(w_ukv.shape, const3),
                  pl.BlockSpec((tm, LANES), tab),
                  pl.BlockSpec((tm, LANES), tab)],
        out_specs=[pl.BlockSpec((heads, tm, nope + rope_d), head_rows),
                   pl.BlockSpec((heads, tm, nope + rope_d), head_rows),
                   pl.BlockSpec((heads, tm, v_dim), head_rows)],
        out_shape=[jax.ShapeDtypeStruct((heads, t, nope + rope_d), BF16),
                   jax.ShapeDtypeStruct((heads, t, nope + rope_d), BF16),
                   jax.ShapeDtypeStruct((heads, t, v_dim), BF16)],
        compiler_params=_params("parallel"),
        name="qkv_up",
    )(qa, rows, w_uq_pad, w_ukv, cos, sin)


def _proj_b_kernel(x_ref, g_ref, w_ref, cos_ref, sin_ref,
                   q_ref, k_ref, v_ref, kb_ref, vb_ref, memq_ref, *, q_scale, mem_scale):
    q_cols, k_cols, v_cols = q_ref.shape[-1], k_ref.shape[-1], v_ref.shape[-1]
    h = _rms(x_ref[...], g_ref[...]).astype(BF16)
    y = jnp.dot(h, w_ref[...], preferred_element_type=F32)
    cos, sin = cos_ref[...], sin_ref[...]
    for c in range(q_cols // LANES):
        sl = slice(c * LANES, (c + 1) * LANES)
        q_ref[:, sl] = (_rope128(y[:, sl], cos, sin) * q_scale).astype(BF16)
    for c in range(k_cols // LANES):
        kr = _rope128(y[:, q_cols + c * LANES:q_cols + (c + 1) * LANES], cos, sin)
        k_ref[:, c * LANES:(c + 1) * LANES] = kr
        kb_ref[:, c * LANES:(c + 1) * LANES] = kr.astype(BF16)
    v = y[:, q_cols + k_cols:q_cols + k_cols + v_cols]
    v_ref[...] = v
    vb_ref[...] = v.astype(BF16)
    memq_ref[...] = y[:, q_cols + k_cols + v_cols:] * mem_scale


def _proj_b(x, g0, w_bf, cos, sin, *, q_cols, k_cols, v_cols, mem_w, q_scale, mem_scale, tm):
    t, d = x.shape
    n_tab = cos.shape[0] // tm
    row = lambda i: (i, 0)
    const = lambda i: (0, 0)
    tab = lambda i: (i % n_tab, 0)
    widths = [(q_cols, BF16), (k_cols, F32), (v_cols, F32), (k_cols, BF16), (v_cols, BF16), (mem_w, F32)]
    return pl.pallas_call(
        functools.partial(_proj_b_kernel, q_scale=q_scale, mem_scale=mem_scale),
        grid=(t // tm,),
        in_specs=[pl.BlockSpec((tm, d), row),
                  pl.BlockSpec((1, d), const),
                  pl.BlockSpec(w_bf.shape, const),
                  pl.BlockSpec((tm, LANES), tab),
                  pl.BlockSpec((tm, LANES), tab)],
        out_specs=[pl.BlockSpec((tm, w), row) for w, _ in widths],
        out_shape=[jax.ShapeDtypeStruct((t, w), dt) for w, dt in widths],
        compiler_params=_params("parallel"),
        name="proj_b",
    )(x, g0.reshape(1, d), w_bf, cos, sin)


def _causal_pairs(nq, tq, tk):
    qi, kj, fl = [], [], []
    for i in range(nq):
        last = (i * tq + tq - 1) // tk
        for j in range(last + 1):
            masked = (j * tk + tk - 1) > i * tq
            qi.append(i)
            kj.append(j)
            fl.append(int(masked) + 2 * int(j == last))
    return (jnp.asarray(qi, jnp.int32), jnp.asarray(kj, jnp.int32), jnp.asarray(fl, jnp.int32))


def _causal_step(q_of, k_of, v_of, n_heads, qi, kj, flags, m_ref, l_ref, acc_ref, *, tq, tk):
    def run(masked):
        for h in range(n_heads):
            s = lax.dot_general(q_of(h), k_of(h), NT_DIMS, preferred_element_type=F32)
            if masked:
                row = lax.broadcasted_iota(jnp.int32, s.shape, 0) % tq + qi * tq
                col = lax.broadcasted_iota(jnp.int32, s.shape, 1) + kj * tk
                s = jnp.where(col <= row, s, NEG_INF)
            _online_softmax_update(s, lambda p, h=h: jnp.dot(p, v_of(h), preferred_element_type=F32),
                                   m_ref.at[h], l_ref.at[h], acc_ref.at[h])

    pl.when(flags % 2 == 1)(functools.partial(run, True))
    pl.when(flags % 2 == 0)(functools.partial(run, False))


def _init_softmax(m_ref, l_ref, acc_ref):
    m_ref[...] = jnp.full(m_ref.shape, NEG_INF, F32)
    l_ref[...] = jnp.zeros(l_ref.shape, F32)
    acc_ref[...] = jnp.zeros(acc_ref.shape, F32)


def _mla_flash_kernel(qi_ref, kj_ref, fl_ref, q_ref, k_ref, v_ref, o_ref, m_ref, l_ref, acc_ref,
                      *, tq, tk):
    p = pl.program_id(2)
    qi, kj, flags = qi_ref[p], kj_ref[p], fl_ref[p]
    hb, _, dv = v_ref.shape

    @pl.when(kj == 0)
    def _init():
        _init_softmax(m_ref, l_ref, acc_ref)

    _causal_step(lambda h: q_ref[h], lambda h: k_ref[h], lambda h: v_ref[h], hb, qi, kj, flags,
                 m_ref, l_ref, acc_ref, tq=tq, tk=tk)

    @pl.when(flags >= 2)
    def _finish():
        for h in range(hb):
            o_ref[:, h * dv:(h + 1) * dv] = _normalized(acc_ref.at[h], l_ref.at[h]).astype(o_ref.dtype)


def _mla_flash(q, k, v, *, batch, tq, tk, hb):
    heads, t, dk = q.shape
    dv = v.shape[-1]
    s = t // batch
    nq, nk = s // tq, s // tk
    qi, kj, fl = _causal_pairs(nq, tq, tk)
    grid_spec = pltpu.PrefetchScalarGridSpec(
        num_scalar_prefetch=3,
        grid=(batch, heads // hb, qi.shape[0]),
        in_specs=[pl.BlockSpec((hb, tq, dk), lambda b, h, p, qi, kj, fl: (h, b * nq + qi[p], 0)),
                  pl.BlockSpec((hb, tk, dk), lambda b, h, p, qi, kj, fl: (h, b * nk + kj[p], 0)),
                  pl.BlockSpec((hb, tk, dv), lambda b, h, p, qi, kj, fl: (h, b * nk + kj[p], 0))],
        out_specs=pl.BlockSpec((tq, hb * dv), lambda b, h, p, qi, kj, fl: (b * nq + qi[p], h)),
        scratch_shapes=[pltpu.VMEM((hb, tq, LANES), F32), pltpu.VMEM((hb, tq, LANES), F32),
                        pltpu.VMEM((hb, tq, dv), F32)],
    )
    return pl.pallas_call(
        functools.partial(_mla_flash_kernel, tq=tq, tk=tk),
        grid_spec=grid_spec,
        out_shape=jax.ShapeDtypeStruct((t, heads * dv), BF16),
        compiler_params=_params("parallel", "parallel", "arbitrary"),
        name="mla_flash",
    )(qi, kj, fl, q, k, v)


def _split_maps(q, head_dim):
    lane = lax.broadcasted_iota(jnp.int32, q.shape, 1)
    zero = jnp.zeros_like(q)
    return jnp.concatenate([jnp.where(lane < head_dim, q, zero), jnp.where(lane >= head_dim, q, zero)], axis=0)


def _diff_finish(o, tq, lam, g_sub, lam_init):
    d = o[:tq] - lam * o[tq:]
    return _rms(d, g_sub) * (1.0 - lam_init)


def _diff_flash_kernel(qi_ref, kj_ref, fl_ref, q_ref, k_ref, v_ref, lam_ref, gsub_ref, o_ref,
                       qs_ref, m_ref, l_ref, acc_ref, *, tq, tk, head_dim, lam_init):
    p = pl.program_id(2)
    qi, kj, flags = qi_ref[p], kj_ref[p], fl_ref[p]

    hb, _, dk = qs_ref.shape
    dv = v_ref.shape[-1]

    @pl.when(kj == 0)
    def _init():
        _init_softmax(m_ref, l_ref, acc_ref)
        for h in range(hb):
            qs_ref[h] = _split_maps(q_ref[:, h * dk:(h + 1) * dk], head_dim)

    _causal_step(lambda h: qs_ref[h], lambda h: k_ref[...], lambda h: v_ref[...], hb, qi, kj, flags,
                 m_ref, l_ref, acc_ref, tq=tq, tk=tk)

    @pl.when(flags >= 2)
    def _finish():
        lam = _diff_lambda(lam_ref, lam_init)
        for h in range(hb):
            o = _normalized(acc_ref.at[h], l_ref.at[h])
            o_ref[:, h * dv:(h + 1) * dv] = _diff_finish(o, tq, lam, gsub_ref[...], lam_init).astype(o_ref.dtype)


def _diff_flash(q, k, v, lam_p, g_sub, *, batch, rep, lam_init, tq, tk, hb):
    t, q_cols = q.shape
    dv = g_sub.shape[-1]
    dk = dv
    heads = q_cols // dk
    assert rep % hb == 0
    s = t // batch
    nq, nk = s // tq, s // tk
    qi, kj, fl = _causal_pairs(nq, tq, tk)
    grid_spec = pltpu.PrefetchScalarGridSpec(
        num_scalar_prefetch=3,
        grid=(batch, heads // hb, qi.shape[0]),
        in_specs=[pl.BlockSpec((tq, hb * dk), lambda b, h, p, qi, kj, fl: (b * nq + qi[p], h)),
                  pl.BlockSpec((tk, dk), lambda b, h, p, qi, kj, fl: (b * nk + kj[p], h * hb // rep)),
                  pl.BlockSpec((tk, dv), lambda b, h, p, qi, kj, fl: (b * nk + kj[p], h * hb // rep)),
                  pl.BlockSpec(lam_p.shape, lambda b, h, p, qi, kj, fl: (0, 0)),
                  pl.BlockSpec((1, dv), lambda b, h, p, qi, kj, fl: (0, 0))],
        out_specs=pl.BlockSpec((tq, hb * dv), lambda b, h, p, qi, kj, fl: (b * nq + qi[p], h)),
        scratch_shapes=[pltpu.VMEM((hb, 2 * tq, dk), BF16), pltpu.VMEM((hb, 2 * tq, LANES), F32),
                        pltpu.VMEM((hb, 2 * tq, LANES), F32), pltpu.VMEM((hb, 2 * tq, dv), F32)],
    )
    return pl.pallas_call(
        functools.partial(_diff_flash_kernel, tq=tq, tk=tk, head_dim=dk // 2, lam_init=lam_init),
        grid_spec=grid_spec,
        out_shape=jax.ShapeDtypeStruct((t, heads * dv), BF16),
        compiler_params=_params("parallel", "parallel", "arbitrary"),
        name="diff_flash",
    )(qi, kj, fl, q, k, v, lam_p, g_sub.reshape(1, dv))


def _mem_attn_kernel(q_ref, k_ref, v_ref, o_ref):
    q = q_ref[...].astype(BF16)
    k = k_ref[...].astype(BF16)
    v = v_ref[...].astype(BF16)
    s = jnp.einsum('bqd,bkd->bqk', q, k, preferred_element_type=F32)
    m = jnp.max(s, axis=-1, keepdims=True)
    p = jnp.exp2(s - m)
    l = jnp.sum(p, axis=-1, keepdims=True)
    o = jnp.einsum('bqk,bkd->bqd', p.astype(BF16), v, preferred_element_type=F32)
    o_ref[...] = o / l


def _mem_attn(q, mem_k, mem_v, layer, *, heads, bb, ts):
    b, s, w = q.shape
    dh = w // heads
    n_mem = mem_k.shape[2]
    kv_spec = pl.BlockSpec((None, bb, n_mem, dh), lambda i, j, h: (layer, i, 0, h))
    return pl.pallas_call(
        _mem_attn_kernel,
        grid=(b // bb, s // ts, heads),
        in_specs=[pl.BlockSpec((bb, ts, dh), lambda i, j, h: (i, j, h)), kv_spec, kv_spec],
        out_specs=pl.BlockSpec((bb, ts, dh), lambda i, j, h: (i, j, h)),
        out_shape=jax.ShapeDtypeStruct((b, s, w), F32),
        compiler_params=_params("parallel", "parallel", "parallel"),
        name="mem_attn",
    )(q, mem_k, mem_v)


def _mem_attn_cached_kernel(q_ref, k_ref, v_ref, o_ref, *, heads, n_mem):
    dh = k_ref.shape[-1]
    for h in range(heads):
        q = q_ref[:, :, h * dh:(h + 1) * dh].astype(BF16)
        k = k_ref[:, pl.ds(h, n_mem, stride=heads), :].astype(BF16)
        v = v_ref[:, pl.ds(h, n_mem, stride=heads), :].astype(BF16)
        s = jnp.einsum('bqd,bkd->bqk', q, k, preferred_element_type=F32)
        m = jnp.max(s, axis=-1, keepdims=True)
        p = jnp.exp2(s - m)
        l = jnp.sum(p, axis=-1, keepdims=True)
        o = jnp.einsum('bqk,bkd->bqd', p.astype(BF16), v, preferred_element_type=F32)
        o_ref[:, :, h * dh:(h + 1) * dh] = o / l


def _mem_attn_cached(q, mem_k, mem_v, layer, *, heads, bb):
    b, s, w = q.shape
    rows, dh = mem_k.shape[2], mem_k.shape[3]
    kv_spec = pl.BlockSpec((None, bb, rows, dh), lambda i: (layer, i, 0, 0))
    return pl.pallas_call(
        functools.partial(_mem_attn_cached_kernel, heads=heads, n_mem=rows // heads),
        grid=(b // bb,),
        in_specs=[pl.BlockSpec((bb, s, w), lambda i: (i, 0, 0)), kv_spec, kv_spec],
        out_specs=pl.BlockSpec((bb, s, w), lambda i: (i, 0, 0)),
        out_shape=jax.ShapeDtypeStruct((b, s, w), F32),
        compiler_params=_params("parallel"),
        name="mem_attn_cached",
    )(q, mem_k, mem_v)


def _out_proj_kernel(main_ref, mem_ref, w1_ref, w2_ref, x_ref, g_ref, o_ref):
    mix = jnp.dot(main_ref[...].astype(BF16), w1_ref[...], preferred_element_type=F32)
    mix = mix + jnp.dot(mem_ref[...].astype(BF16), w2_ref[...], preferred_element_type=F32)
    o_ref[...] = x_ref[...] + _rms(mix, g_ref[...])


def _out_proj(main, mem_out, w_main, w_mem, x, g1, *, tm):
    t, d = x.shape
    row = lambda i: (i, 0)
    const = lambda i: (0, 0)
    return pl.pallas_call(
        _out_proj_kernel,
        grid=(t // tm,),
        in_specs=[pl.BlockSpec((tm, main.shape[1]), row),
                  pl.BlockSpec((tm, mem_out.shape[1]), row),
                  pl.BlockSpec(w_main.shape, const),
                  pl.BlockSpec(w_mem.shape, const),
                  pl.BlockSpec((tm, d), row),
                  pl.BlockSpec((1, d), const)],
        out_specs=pl.BlockSpec((tm, d), row),
        out_shape=jax.ShapeDtypeStruct((t, d), F32),
        compiler_params=_params("parallel"),
        name="out_proj",
    )(main, mem_out, w_main, w_mem, x, g1.reshape(1, d))


def _ffn_kernel(x_ref, g2_ref, g3_ref, wu_ref, wd_ref, o_ref, h_ref, acc_ref):
    j = pl.program_id(1)

    @pl.when(j == 0)
    def _init():
        h_ref[...] = _rms(x_ref[...], g2_ref[...]).astype(BF16)
        acc_ref[...] = jnp.zeros(acc_ref.shape, F32)

    u = jnp.maximum(jnp.dot(h_ref[...], wu_ref[...], preferred_element_type=F32), 0.0)
    acc_ref[...] += jnp.dot((u * u).astype(BF16), wd_ref[...], preferred_element_type=F32)

    @pl.when(j == pl.num_programs(1) - 1)
    def _finish():
        o_ref[...] = x_ref[...] + _rms(acc_ref[...], g3_ref[...])


def _ffn(x, g2, g3, w_up, w_down, layer, *, tm, tf):
    t, d = x.shape
    f = w_up.shape[2]
    rows_mode = dict(pipeline_mode=pl.Buffered(1)) if t == tm else {}
    return pl.pallas_call(
        _ffn_kernel,
        grid=(t // tm, f // tf),
        in_specs=[pl.BlockSpec((tm, d), lambda i, j: (i, 0), **rows_mode),
                  pl.BlockSpec((1, d), lambda i, j: (0, 0)),
                  pl.BlockSpec((1, d), lambda i, j: (0, 0)),
                  pl.BlockSpec((None, d, tf), lambda i, j: (layer, 0, j)),
                  pl.BlockSpec((None, tf, d), lambda i, j: (layer, j, 0))],
        out_specs=pl.BlockSpec((tm, d), lambda i, j: (i, 0), **rows_mode),
        out_shape=jax.ShapeDtypeStruct((t, d), F32),
        scratch_shapes=[pltpu.VMEM((tm, d), BF16), pltpu.VMEM((tm, d), F32)],
        compiler_params=_params("parallel", "arbitrary"),
        name="ffn",
    )(x, g2.reshape(1, d), g3.reshape(1, d), w_up, w_down)


def _q_abs_kernel(qa_ref, wq_ref, wuk_ref, cos_ref, sin_ref, o_ref, *, nope, kv_rank, scale):
    bb, t_new, width = o_ref.shape
    r = jnp.dot(qa_ref[...], wq_ref[...], preferred_element_type=F32)
    q_lat = lax.dot_general(r[:, :nope].astype(BF16), wuk_ref[...], NT_DIMS, preferred_element_type=F32)
    rp = _rope128(r[:, nope:nope + LANES], cos_ref[...], sin_ref[...])
    o_ref[:, :, :kv_rank] = (q_lat * scale).reshape(bb, t_new, kv_rank)
    o_ref[:, :, kv_rank:] = (rp[:, :width - kv_rank] * scale).reshape(bb, t_new, width - kv_rank)


def _q_abs(qa, w_uq_pad, w_uk2d, cos, sin, *, t_new, nope, kv_rank, rope_d, scale, bb):
    t, q_rank = qa.shape
    heads, _, wcols = w_uq_pad.shape
    nb = t // t_new
    tm = bb * t_new
    return pl.pallas_call(
        functools.partial(_q_abs_kernel, nope=nope, kv_rank=kv_rank, scale=scale),
        grid=(nb // bb, heads),
        in_specs=[pl.BlockSpec((tm, q_rank), lambda i, h: (i, 0)),
                  pl.BlockSpec((None, q_rank, wcols), lambda i, h: (h, 0, 0)),
                  pl.BlockSpec((kv_rank, nope), lambda i, h: (0, h)),
                  pl.BlockSpec((tm, LANES), lambda i, h: (0, 0)),
                  pl.BlockSpec((tm, LANES), lambda i, h: (0, 0))],
        out_specs=pl.BlockSpec((bb, t_new, kv_rank + rope_d), lambda i, h: (i, h, 0)),
        out_shape=jax.ShapeDtypeStruct((nb, heads * t_new, kv_rank + rope_d), F32),
        compiler_params=_params("parallel", "parallel"),
        name="q_abs",
    )(qa, w_uq_pad, w_uk2d, cos, sin)


def _new_token_scores(q, kn_ref, t_new):
    s = lax.dot_general(q, kn_ref[...], NT_DIMS, preferred_element_type=F32)
    row_t = lax.broadcasted_iota(jnp.int32, s.shape, 0) % t_new
    col = lax.broadcasted_iota(jnp.int32, s.shape, 1)
    return jnp.where(col <= row_t, s, NEG_INF)


def _paged_mla_kernel(pt_ref, q_ref, new_ref, *rest, nbb, n_pp, page, kv_rank, t_new):
    pages = rest[:nbb * n_pp]
    o_ref, qb_ref, kb_ref, kn_ref, m_ref, l_ref, acc_ref = rest[nbb * n_pp:]
    j = pl.program_id(1)

    @pl.when(j == 0)
    def _init():
        _init_softmax(m_ref, l_ref, acc_ref)
        qb_ref[...] = q_ref[...].astype(BF16)

    for i in range(nbb):
        for k in range(n_pp):
            kb_ref[i, :, k * page:(k + 1) * page] = pages[i * n_pp + k][...].astype(BF16)
        s = jnp.dot(qb_ref[i], kb_ref[i], preferred_element_type=F32)
        _online_softmax_update(
            s, lambda p, i=i: lax.dot_general(p, kb_ref[i, :kv_rank, :], NT_DIMS, preferred_element_type=F32),
            m_ref.at[i], l_ref.at[i], acc_ref.at[i])

    @pl.when(j == pl.num_programs(1) - 1)
    def _finish():
        for i in range(nbb):
            kn_ref[...] = jnp.zeros(kn_ref.shape, BF16)
            kn_ref[:t_new, :] = new_ref[i].astype(BF16)
            s_new = _new_token_scores(qb_ref[i], kn_ref, t_new)
            _online_softmax_update(
                s_new, lambda p: jnp.dot(p, kn_ref[:, :kv_rank], preferred_element_type=F32),
                m_ref.at[i], l_ref.at[i], acc_ref.at[i])
            o_ref[i] = _normalized(acc_ref.at[i], l_ref.at[i])


def _paged_mla(q_abs, rows_new, cache_t, layer, page_table, *, kv_rank, n_pp, nbb):
    nb, q_rows, width = q_abs.shape
    t_new = rows_new.shape[1]
    page = cache_t.shape[3]
    n_pages = page_table.shape[1]

    def page_spec(i, k):
        return pl.BlockSpec((None, None, width, page),
                            lambda b, j, pt: (layer, pt[b * nbb + i, j * n_pp + k], 0, 0))

    pad_rows = page
    grid_spec = pltpu.PrefetchScalarGridSpec(
        num_scalar_prefetch=1,
        grid=(nb // nbb, n_pages // n_pp),
        in_specs=[pl.BlockSpec((nbb, q_rows, width), lambda b, j, pt: (b, 0, 0)),
                  pl.BlockSpec((nbb, t_new, width), lambda b, j, pt: (b, 0, 0))]
                 + [page_spec(i, k) for i in range(nbb) for k in range(n_pp)],
        out_specs=pl.BlockSpec((nbb, q_rows, kv_rank), lambda b, j, pt: (b, 0, 0)),
        scratch_shapes=[pltpu.VMEM((nbb, q_rows, width), BF16),
                        pltpu.VMEM((nbb, width, n_pp * page), BF16),
                        pltpu.VMEM((pad_rows, width), BF16),
                        pltpu.VMEM((nbb, q_rows, LANES), F32), pltpu.VMEM((nbb, q_rows, LANES), F32),
                        pltpu.VMEM((nbb, q_rows, kv_rank), F32)],
    )
    return pl.pallas_call(
        functools.partial(_paged_mla_kernel, nbb=nbb, n_pp=n_pp, page=page, kv_rank=kv_rank, t_new=t_new),
        grid_spec=grid_spec,
        out_shape=jax.ShapeDtypeStruct((nb, q_rows, kv_rank), F32),
        compiler_params=_params("parallel", "arbitrary"),
        name="paged_mla",
    )(page_table, q_abs, rows_new, *([cache_t] * (nbb * n_pp)))


def _o_up_kernel(o_ref, w_ref, out_ref):
    bb, t_new, c = o_ref.shape
    o = o_ref[...].reshape(bb * t_new, c).astype(BF16)
    out_ref[...] = jnp.dot(o, w_ref[...], preferred_element_type=F32).astype(out_ref.dtype)


def _o_up(o_lat, w_uv2d, *, t_new, v_dim, bb):
    nb, q_rows, kv_rank = o_lat.shape
    heads = q_rows // t_new
    return pl.pallas_call(
        _o_up_kernel,
        grid=(nb // bb, heads),
        in_specs=[pl.BlockSpec((bb, t_new, kv_rank), lambda i, h: (i, h, 0)),
                  pl.BlockSpec((kv_rank, v_dim), lambda i, h: (0, h))],
        out_specs=pl.BlockSpec((bb * t_new, v_dim), lambda i, h: (i, h)),
        out_shape=jax.ShapeDtypeStruct((nb * t_new, heads * v_dim), BF16),
        compiler_params=_params("parallel", "parallel"),
        name="o_up",
    )(o_lat, w_uv2d)


def _paged_diff_kernel(pt_ref, q_ref, kn_in_ref, vn_in_ref, lam_ref, gsub_ref, *rest,
                       nbb, n_pp, page, kv_heads, t_new, lam_init):
    kpages, vpages = rest[:nbb * n_pp], rest[nbb * n_pp:2 * nbb * n_pp]
    o_ref, kb_ref, vb_ref, kn_ref, vn_ref, m_ref, l_ref, acc_ref = rest[2 * nbb * n_pp:]
    j = pl.program_id(1)
    dv = acc_ref.shape[-1]
    rows_g = q_ref.shape[1] // kv_heads

    def update(i, s, v_of_head):
        def pv_fn(p):
            return jnp.concatenate(
                [jnp.dot(p[g * rows_g:(g + 1) * rows_g], v_of_head(g), preferred_element_type=F32)
                 for g in range(kv_heads)], axis=0)
        _online_softmax_update(s, pv_fn, m_ref.at[i], l_ref.at[i], acc_ref.at[i])

    @pl.when(j == 0)
    def _init():
        _init_softmax(m_ref, l_ref, acc_ref)

    for i in range(nbb):
        for k in range(n_pp):
            kb_ref[i, :, k * page:(k + 1) * page] = kpages[i * n_pp + k][...].astype(BF16)
            for g in range(kv_heads):
                vb_ref[i, g, k * page:(k + 1) * page, :] = (
                    vpages[i * n_pp + k][pl.ds(g, page, stride=kv_heads), :].astype(BF16))
        s = jnp.dot(q_ref[i], kb_ref[i], preferred_element_type=F32)
        update(i, s, lambda g: vb_ref[i, g])

    @pl.when(j == pl.num_programs(1) - 1)
    def _finish():
        lam = _diff_lambda(lam_ref, lam_init)
        half = rows_g // 2
        for i in range(nbb):
            kn_ref[...] = jnp.zeros(kn_ref.shape, BF16)
            vn_ref[...] = jnp.zeros(vn_ref.shape, BF16)
            kn_ref[:t_new, :] = kn_in_ref[i].astype(BF16)
            vn_ref[:t_new, :] = vn_in_ref[i].astype(BF16)
            update(i, _new_token_scores(q_ref[i], kn_ref, t_new), lambda g: vn_ref[:, g * dv:(g + 1) * dv])
            o = _normalized(acc_ref.at[i], l_ref.at[i])
            for g in range(kv_heads):
                og = o[g * rows_g:(g + 1) * rows_g]
                o_ref[i, g * half:(g + 1) * half, :] = _diff_finish(og, half, lam, gsub_ref[...], lam_init)


def _paged_diff(q_bd, k_new, v_new, cache_kt, cache_v2, layer, page_table, lam_p, g_sub,
                *, kv_heads, lam_init, n_pp, nbb):
    nb, q_rows, kw = q_bd.shape
    t_new = k_new.shape[1]
    page = cache_kt.shape[3]
    n_pages = page_table.shape[1]
    dv = cache_v2.shape[3]
    vw = dv * kv_heads

    def page_spec(i, k, rows, cols):
        return pl.BlockSpec((None, None, rows, cols),
                            lambda b, j, pt: (layer, pt[b * nbb + i, j * n_pp + k], 0, 0))

    slots = [(i, k) for i in range(nbb) for k in range(n_pp)]
    pad_rows = page
    grid_spec = pltpu.PrefetchScalarGridSpec(
        num_scalar_prefetch=1,
        grid=(nb // nbb, n_pages // n_pp),
        in_specs=[pl.BlockSpec((nbb, q_rows, kw), lambda b, j, pt: (b, 0, 0)),
                  pl.BlockSpec((nbb, t_new, kw), lambda b, j, pt: (b, 0, 0)),
                  pl.BlockSpec((nbb, t_new, vw), lambda b, j, pt: (b, 0, 0)),
                  pl.BlockSpec(lam_p.shape, lambda b, j, pt: (0, 0)),
                  pl.BlockSpec((1, dv), lambda b, j, pt: (0, 0))]
                 + [page_spec(i, k, kw, page) for i, k in slots]
                 + [page_spec(i, k, page * kv_heads, dv) for i, k in slots],
        out_specs=pl.BlockSpec((nbb, q_rows // 2, dv), lambda b, j, pt: (b, 0, 0)),
        scratch_shapes=[pltpu.VMEM((nbb, kw, n_pp * page), BF16),
                        pltpu.VMEM((nbb, kv_heads, n_pp * page, dv), BF16),
                        pltpu.VMEM((pad_rows, kw), BF16), pltpu.VMEM((pad_rows, vw), BF16),
                        pltpu.VMEM((nbb, q_rows, LANES), F32), pltpu.VMEM((nbb, q_rows, LANES), F32),
                        pltpu.VMEM((nbb, q_rows, dv), F32)],
    )
    n_ops = nbb * n_pp
    return pl.pallas_call(
        functools.partial(_paged_diff_kernel, nbb=nbb, n_pp=n_pp, page=page, kv_heads=kv_heads, t_new=t_new,
                          lam_init=lam_init),
        grid_spec=grid_spec,
        out_shape=jax.ShapeDtypeStruct((nb, q_rows // 2, dv), F32),
        compiler_params=_params("parallel", "arbitrary"),
        name="paged_diff",
    )(page_table, q_bd, k_new, v_new, lam_p, g_sub.reshape(1, dv), *([cache_kt] * n_ops), *([cache_v2] * n_ops))


def _largest_divisor(n, cap):
    d = min(n, cap)
    while n % d:
        d -= 1
    return d


def kernel(x_prompt, x_sample, mem_prompt, cache_mla, cache_diff_k, cache_diff_v, cache_mem_k, cache_mem_v,
           page_table, norm_gains, w_mix_in_a, g_mla_q, g_mla_kv, w_mla_uq, w_mla_uk, w_mla_uv, w_mix_in_b,
           diff_lambda, g_diff_sub, g_mem, w_mem_kv, w_mix_out, w_ff_up, w_ff_down):
    batch, seq, d_model = x_prompt.shape
    dec_batch, dec_seq, _ = x_sample.shape
    depth = norm_gains.shape[0]
    n_mem = mem_prompt.shape[1]
    mem_heads, mem_hd = cache_mem_k.shape[3], cache_mem_k.shape[4]
    mem_w = mem_heads * mem_hd
    mem_scale = LOG2_E * mem_hd ** -0.5
    page = cache_mla.shape[2]
    n_pages = page_table.shape[1]
    past_len = n_pages * page

    q_rank, kv_rank = g_mla_q.shape[1], g_mla_kv.shape[1]
    mla_heads, nope = w_mla_uk.shape[2], w_mla_uk.shape[3]
    mla_v = w_mla_uv.shape[3]
    rope_d = cache_mla.shape[3] - kv_rank
    mla_scale = LOG2_E * (nope + rope_d) ** -0.5
    kv_heads, head_dim = cache_diff_k.shape[3], cache_diff_k.shape[5]
    diff_v = cache_diff_v.shape[4]
    k_cols, v_cols = kv_heads * 2 * head_dim, kv_heads * diff_v
    q_cols = w_mix_in_b.shape[2] - k_cols - v_cols - mem_w
    diff_heads = q_cols // (2 * head_dim)
    rep = diff_heads // kv_heads
    diff_scale = LOG2_E * head_dim ** -0.5
    assert rope_d == head_dim == LANES // 2 and diff_v == 2 * head_dim == nope == mla_v == mem_hd == LANES

    cos_p, sin_p = _rope_tables(jnp.arange(seq, dtype=jnp.int32), rope_d // 2)
    cos_s, sin_s = _rope_tables(past_len + jnp.arange(dec_seq, dtype=jnp.int32), rope_d // 2)
    t_s = dec_batch * dec_seq
    tm_s = _largest_divisor(t_s, 256)
    bb_s = tm_s // dec_seq
    cos_s, sin_s = jnp.tile(cos_s, (bb_s, 1)), jnp.tile(sin_s, (bb_s, 1))

    tm_p = _largest_divisor(seq, 256)
    t_att = _largest_divisor(seq, 512)
    n_pp = _largest_divisor(n_pages, 16)
    nbb = _largest_divisor(dec_batch, 2)

    mem_k_p, mem_v_p = _mem_kv(mem_prompt.reshape(batch * n_mem, d_model), g_mem, w_mem_kv.astype(BF16))
    mem_k_p = mem_k_p.reshape(depth, batch, n_mem, mem_w)
    mem_v_p = mem_v_p.reshape(depth, batch, n_mem, mem_w)
    mem_k_s = cache_mem_k.reshape(depth, dec_batch, n_mem * mem_heads, mem_hd)
    mem_v_s = cache_mem_v.reshape(depth, dec_batch, n_mem * mem_heads, mem_hd)
    page_table = page_table.astype(jnp.int32)
    w_up_bf, w_down_bf = w_ff_up.astype(BF16), w_ff_down.astype(BF16)
    tf = _largest_divisor(w_ff_up.shape[2], 512)

    xp = x_prompt.reshape(batch * seq, d_model)
    xs = x_sample.reshape(t_s, d_model)
    rows_a_p, rows_a_s, rows_k_p, rows_k_s, rows_v_p, rows_v_s = [], [], [], [], [], []

    for i in range(depth):
        j = i // N_MIXERS
        w_out = w_mix_out[i].astype(BF16)
        w_out_main, w_out_mem = w_out[:d_model - mem_w], w_out[d_model - mem_w:]
        if i % N_MIXERS == 0:
            w_in = w_mix_in_a[j]
            o = q_rank + kv_rank
            w_ext = jnp.concatenate(
                [w_in[:, :o], w_in[:, o + rope_d:], w_in[:, o:o + rope_d],
                 jnp.zeros((d_model, LANES - rope_d), F32)], axis=1).astype(BF16)
            w_uq = w_mla_uq[j].reshape(q_rank, mla_heads, nope + rope_d)
            w_uq_pad = jnp.pad(jnp.transpose(w_uq, (1, 0, 2)),
                               ((0, 0), (0, 0), (0, nope + LANES - nope - rope_d))).astype(BF16)
            w_uk2d = w_mla_uk[j].reshape(kv_rank, mla_heads * nope).astype(BF16)
            w_uv2d = w_mla_uv[j].reshape(kv_rank, mla_heads * mla_v).astype(BF16)
            w_ukv = jnp.transpose(jnp.concatenate([w_mla_uk[j], w_mla_uv[j]], axis=-1), (1, 0, 2)).astype(BF16)
            proj = functools.partial(_proj_a, g0=norm_gains[i, 0], w_ext=w_ext, g_q=g_mla_q[j], g_kv=g_mla_kv[j],
                                     q_rank=q_rank, kv_rank=kv_rank, rope_d=rope_d, mem_w=mem_w,
                                     mem_scale=mem_scale)
            qa, rows_p, memq_p = proj(xp, cos=cos_p, sin=sin_p, tm=tm_p)
            q, k, v = _qkv_up(qa, rows_p, w_uq_pad, w_ukv, cos_p, sin_p, nope=nope, kv_rank=kv_rank,
                              scale=mla_scale, tm=tm_p)
            main_p = _mla_flash(q, k, v, batch=batch, tq=t_att, tk=t_att, hb=_largest_divisor(mla_heads, 6))
            rows_a_p.append(rows_p.reshape(batch, seq, kv_rank + rope_d))
            qa, rows_s, memq_s = proj(xs, cos=cos_s, sin=sin_s, tm=tm_s)
            q_abs = _q_abs(qa, w_uq_pad, w_uk2d, cos_s, sin_s, t_new=dec_seq, nope=nope, kv_rank=kv_rank,
                           rope_d=rope_d, scale=mla_scale, bb=bb_s)
            rows_s3 = rows_s.reshape(dec_batch, dec_seq, kv_rank + rope_d)
            o_lat = _paged_mla(q_abs, rows_s3, jnp.swapaxes(cache_mla, 2, 3), j, page_table,
                               kv_rank=kv_rank, n_pp=n_pp, nbb=nbb)
            main_s = _o_up(o_lat, w_uv2d, t_new=dec_seq, v_dim=mla_v, bb=bb_s)
            rows_a_s.append(rows_s3)
        else:
            lam_init = _lambda_init(i)
            proj = functools.partial(_proj_b, g0=norm_gains[i, 0], w_bf=w_mix_in_b[j].astype(BF16),
                                     q_cols=q_cols, k_cols=k_cols, v_cols=v_cols, mem_w=mem_w,
                                     q_scale=diff_scale, mem_scale=mem_scale)
            q, k_p, v_p, kb, vb, memq_p = proj(xp, cos=cos_p, sin=sin_p, tm=tm_p)
            main_p = _diff_flash(q, kb, vb, diff_lambda[j], g_diff_sub[j], batch=batch, rep=rep,
                                 lam_init=lam_init, tq=t_att, tk=t_att, hb=_largest_divisor(rep, 6))
            rows_k_p.append(k_p.reshape(batch, seq, kv_heads, 2, head_dim))
            rows_v_p.append(v_p.reshape(batch, seq, kv_heads, diff_v))
            q, k_s, v_s, _, _, memq_s = proj(xs, cos=cos_s, sin=sin_s, tm=tm_s)
            q6 = q.reshape(dec_batch, dec_seq, kv_heads, rep, 2, head_dim)
            q6 = jnp.transpose(q6, (0, 2, 4, 3, 1, 5)).reshape(dec_batch, 2 * kv_heads, rep * dec_seq, head_dim)
            eye = jnp.eye(2 * kv_heads, dtype=BF16)
            q_bd = (q6[:, :, :, None, :] * eye[None, :, None, :, None]).reshape(
                dec_batch, 2 * kv_heads * rep * dec_seq, k_cols)
            n_phys = cache_diff_k.shape[1]
            ck = jnp.transpose(cache_diff_k, (0, 1, 3, 4, 5, 2)).reshape(-1, n_phys, k_cols, page)
            cv = cache_diff_v.reshape(-1, n_phys, page * kv_heads, diff_v)
            o_s = _paged_diff(q_bd, k_s.reshape(dec_batch, dec_seq, k_cols), v_s.reshape(dec_batch, dec_seq, v_cols),
                              ck, cv, j, page_table, diff_lambda[j], g_diff_sub[j],
                              kv_heads=kv_heads, lam_init=lam_init, n_pp=n_pp, nbb=nbb)
            main_s = jnp.transpose(o_s.reshape(dec_batch, kv_heads, rep, dec_seq, diff_v),
                                   (0, 3, 1, 2, 4)).reshape(t_s, diff_heads * diff_v)
            rows_k_s.append(k_s.reshape(dec_batch, dec_seq, kv_heads, 2, head_dim))
            rows_v_s.append(v_s.reshape(dec_batch, dec_seq, kv_heads, diff_v))

        mem_p = _mem_attn(memq_p.reshape(batch, seq, mem_w), mem_k_p, mem_v_p, i, heads=mem_heads,
                          bb=1, ts=_largest_divisor(seq, 1024))
        mem_s = _mem_attn_cached(memq_s.reshape(dec_batch, dec_seq, mem_w), mem_k_s, mem_v_s, i, heads=mem_heads,
                                 bb=_largest_divisor(dec_batch, 8))
        xp = _out_proj(main_p, mem_p.reshape(batch * seq, mem_w), w_out_main, w_out_mem, xp, norm_gains[i, 1], tm=tm_p)
        xp = _ffn(xp, norm_gains[i, 2], norm_gains[i, 3], w_up_bf, w_down_bf, i,
                  tm=_largest_divisor(batch * seq, 512), tf=tf)
        xs = _out_proj(main_s, mem_s.reshape(t_s, mem_w), w_out_main, w_out_mem, xs, norm_gains[i, 1], tm=tm_s)
        xs = _ffn(xs, norm_gains[i, 2], norm_gains[i, 3], w_up_bf, w_down_bf, i,
                  tm=_largest_divisor(t_s, 1024), tf=tf)

    return (xp.reshape(batch, seq, d_model), xs.reshape(dec_batch, dec_seq, d_model),
            jnp.stack(rows_a_p), jnp.stack(rows_a_s), jnp.stack(rows_k_p), jnp.stack(rows_k_s),
            jnp.stack(rows_v_p), jnp.stack(rows_v_s),
            mem_k_p.reshape(depth, batch, n_mem, mem_heads, mem_hd),
            mem_v_p.reshape(depth, batch, n_mem, mem_heads, mem_hd))
```
